```python
import jax, jax.numpy as jnp
from jax import lax
import numpy as np

D_MODEL = 4096
BATCH = 4
SEQ = 2048
DEPTH = 2

HEAD_DIM = 128
ROPE_THETA = 10000.0
NORM_EPS = 1e-6
PLE_DIM = 256
NEG_INF = -1e30

SWA_HEADS = 12
SWA_KV_HEADS = 4
SWA_GROUP = SWA_HEADS // SWA_KV_HEADS
SWA_WINDOW = 128
SWA_BLOCK = 128
SWA_OUT = SWA_HEADS * HEAD_DIM

RG_WIDTH = 1024
RG_BLOCKS = 8
RG_BLOCK_DIM = RG_WIDTH // RG_BLOCKS
RG_CONV = 4
RG_C = 8.0

MLA_HEADS = 12
MLA_Q_RANK = 1024
MLA_KV_RANK = 512
MLA_NOPE = 128
MLA_ROPE = 64
MLA_V = 128
MLA_QBLOCK = 128
MLA_OUT = MLA_HEADS * MLA_V

MIX_WIDTH = SWA_OUT + RG_WIDTH + MLA_OUT

IN_SIZES = (SWA_OUT, SWA_KV_HEADS * HEAD_DIM, SWA_KV_HEADS * HEAD_DIM,
            RG_WIDTH, RG_WIDTH, MLA_Q_RANK, MLA_KV_RANK, MLA_ROPE)
IN_WIDTH = sum(IN_SIZES)
IN_SPLITS = tuple(int(v) for v in np.cumsum(IN_SIZES)[:-1])

D_FF = -(-8 * D_MODEL // (3 * 256)) * 256

kernel_name = 'hybrid_parallel_heads_swa_rglru_mla'


def rmsnorm(x, g):
    xf = x.astype(jnp.float32)
    y = xf * lax.rsqrt(jnp.mean(xf * xf, axis=-1, keepdims=True) + NORM_EPS)
    return (y * g.astype(jnp.float32)).astype(x.dtype)


def rope(x, positions):
    d = x.shape[-1]
    inv = ROPE_THETA ** (-jnp.arange(0, d, 2, dtype=jnp.float32) / d)
    ang = positions.astype(jnp.float32)[:, :, None] * inv
    cos = jnp.cos(ang)[:, :, None, :]
    sin = jnp.sin(ang)[:, :, None, :]
    xf = x.astype(jnp.float32)
    x1, x2 = jnp.split(xf, 2, axis=-1)
    return jnp.concatenate([x1 * cos - x2 * sin, x2 * cos + x1 * sin], axis=-1).astype(x.dtype)


def swa_attention(q, k, v, sinks):
    B, S = q.shape[0], q.shape[1]
    L = SWA_BLOCK
    nb = S // L
    qb = q.reshape(B, nb, L, SWA_KV_HEADS, SWA_GROUP, HEAD_DIM)

    def with_prev(t):
        t = t.reshape(B, nb, L, SWA_KV_HEADS, HEAD_DIM)
        prev = jnp.pad(t[:, :-1], ((0, 0), (1, 0), (0, 0), (0, 0), (0, 0)))
        return jnp.concatenate([prev, t], axis=2)

    kk, vv = with_prev(k), with_prev(v)
    scores = jnp.einsum('bnqkgd,bnskd->bnkgqs', qb, kk,
                        preferred_element_type=jnp.float32) * (HEAD_DIM ** -0.5)
    blk = jnp.arange(nb)[:, None, None]
    qpos = blk * L + jnp.arange(L)[None, :, None]
    kpos = (blk - 1) * L + jnp.arange(2 * L)[None, None, :]
    valid = (kpos >= 0) & (kpos <= qpos) & (qpos - kpos < SWA_WINDOW)
    scores = jnp.where(valid[None, :, None, None], scores, NEG_INF)
    sink = sinks.astype(jnp.float32).reshape(SWA_KV_HEADS, SWA_GROUP)[None, None, :, :, None, None]
    sink = jnp.broadcast_to(sink, scores.shape[:-1] + (1,))
    probs = jax.nn.softmax(jnp.concatenate([scores, sink], axis=-1), axis=-1)[..., :-1]
    out = jnp.einsum('bnkgqs,bnskd->bnqkgd', probs.astype(v.dtype), vv)
    return out.reshape(B, S, SWA_OUT)


def rglru(xr, conv_w, conv_b, wa, ba, wx, bx, lam):
    B, S = xr.shape[0], xr.shape[1]
    xc = lax.conv_general_dilated(xr, conv_w[:, None, :], window_strides=(1,),
                                  padding=[(RG_CONV - 1, 0)],
                                  dimension_numbers=('NWC', 'WIO', 'NWC'),
                                  feature_group_count=RG_WIDTH) + conv_b
    xg = xc.reshape(B, S, RG_BLOCKS, RG_BLOCK_DIM)
    r = jax.nn.sigmoid((jnp.einsum('bsnc,ncd->bsnd', xg, wa).reshape(B, S, RG_WIDTH) + ba).astype(jnp.float32))
    i = jax.nn.sigmoid((jnp.einsum('bsnc,ncd->bsnd', xg, wx).reshape(B, S, RG_WIDTH) + bx).astype(jnp.float32))
    log_a = -RG_C * jax.nn.softplus(-lam.astype(jnp.float32)) * r
    a = jnp.exp(log_a)
    b = jnp.sqrt(-jnp.expm1(2.0 * log_a)) * i * xc.astype(jnp.float32)

    def combine(left, right):
        a1, b1 = left
        a2, b2 = right
        return a1 * a2, a2 * b1 + b2

    _, h = lax.associative_scan(combine, (a, b), axis=1)
    return h.astype(xr.dtype)


def mla_attention(cq, ckv, kr, positions, q_norm, w_uq, kv_norm, w_ukv):
    B, S = cq.shape[0], cq.shape[1]
    q = (rmsnorm(cq, q_norm) @ w_uq).reshape(B, S, MLA_HEADS, MLA_NOPE + MLA_ROPE)
    q_nope = q[..., :MLA_NOPE]
    q_rope = rope(q[..., MLA_NOPE:], positions)
    kv = (rmsnorm(ckv, kv_norm) @ w_ukv).reshape(B, S, MLA_HEADS, MLA_NOPE + MLA_V)
    k_nope, v = kv[..., :MLA_NOPE], kv[..., MLA_NOPE:]
    k_rope = rope(kr[:, :, None, :], positions)[:, :, 0]
    scale = (MLA_NOPE + MLA_ROPE) ** -0.5
    outs = []
    for j in range(S // MLA_QBLOCK):
        q0 = j * MLA_QBLOCK
        end = q0 + MLA_QBLOCK
        s = (jnp.einsum('bqhd,bkhd->bhqk', q_nope[:, q0:end], k_nope[:, :end],
                        preferred_element_type=jnp.float32)
             + jnp.einsum('bqhd,bkd->bhqk', q_rope[:, q0:end], k_rope[:, :end],
                          preferred_element_type=jnp.float32)) * scale
        causal = jnp.arange(end)[None, :] <= (q0 + jnp.arange(MLA_QBLOCK))[:, None]
        s = jnp.where(causal, s, NEG_INF)
        pr = jax.nn.softmax(s, axis=-1).astype(v.dtype)
        outs.append(jnp.einsum('bhqk,bkhd->bqhd', pr, v[:, :end]))
    return jnp.concatenate(outs, axis=1).reshape(B, S, MLA_OUT)


def setup_inputs(seed: int = 0) -> dict:
    key = jax.random.key(seed)
    ks = iter(jax.random.split(key, 40))

    def nrm(shape, scale):
        return jax.random.normal(next(ks), shape, jnp.float32) * scale

    def gain(n):
        return 1.0 + 0.05 * nrm((DEPTH, n), 1.0)

    x = nrm((BATCH, SEQ, D_MODEL), 1.0)
    p = nrm((DEPTH, BATCH, SEQ, PLE_DIM), 1.0)
    offset = jax.random.randint(next(ks), (BATCH,), 0, 4096, dtype=jnp.int32)
    positions = offset[:, None] + jnp.arange(SEQ, dtype=jnp.int32)[None, :]
    u = jax.random.uniform(next(ks), (DEPTH, RG_WIDTH), jnp.float32, 0.9, 0.999)
    a_base = u ** (1.0 / RG_C)
    rg_lambda = jnp.log(a_base) - jnp.log1p(-a_base)
    return {
        'x': x,
        'p': p,
        'positions': positions,
        'pre_mix_norm': gain(D_MODEL),
        'w_in': nrm((DEPTH, D_MODEL, IN_WIDTH), D_MODEL ** -0.5),
        'swa_sinks': nrm((DEPTH, SWA_HEADS), 1.0),
        'rg_conv_w': nrm((DEPTH, RG_CONV, RG_WIDTH), RG_CONV ** -0.5),
        'rg_conv_b': nrm((DEPTH, RG_WIDTH), 0.02),
        'rg_gate_a_w': nrm((DEPTH, RG_BLOCKS, RG_BLOCK_DIM, RG_BLOCK_DIM), RG_BLOCK_DIM ** -0.5),
        'rg_gate_a_b': nrm((DEPTH, RG_WIDTH), 0.1),
        'rg_gate_x_w': nrm((DEPTH, RG_BLOCKS, RG_BLOCK_DIM, RG_BLOCK_DIM), RG_BLOCK_DIM ** -0.5),
        'rg_gate_x_b': nrm((DEPTH, RG_WIDTH), 0.1),
        'rg_lambda': rg_lambda,
        'mla_q_norm': gain(MLA_Q_RANK),
        'mla_w_uq': nrm((DEPTH, MLA_Q_RANK, MLA_HEADS * (MLA_NOPE + MLA_ROPE)), MLA_Q_RANK ** -0.5),
        'mla_kv_norm': gain(MLA_KV_RANK),
        'mla_w_ukv': nrm((DEPTH, MLA_KV_RANK, MLA_HEADS * (MLA_NOPE + MLA_V)), MLA_KV_RANK ** -0.5),
        'group_norm': gain(MIX_WIDTH),
        'w_out': nrm((DEPTH, MIX_WIDTH, D_MODEL), MIX_WIDTH ** -0.5),
        'post_mix_norm': gain(D_MODEL),
        'pre_ffn_norm': gain(D_MODEL),
        'w_gate': nrm((DEPTH, D_MODEL, D_FF), D_MODEL ** -0.5),
        'w_up': nrm((DEPTH, D_MODEL, D_FF), D_MODEL ** -0.5),
        'w_down': nrm((DEPTH, D_FF, D_MODEL), D_FF ** -0.5),
        'post_ffn_norm': gain(D_MODEL),
        'w_ple': nrm((DEPTH, PLE_DIM, D_MODEL), PLE_DIM ** -0.5),
        'ple_norm': gain(D_MODEL),
        'w_ple_gate': nrm((DEPTH, D_MODEL, D_MODEL), D_MODEL ** -0.5),
        'b_ple_gate': nrm((DEPTH, D_MODEL), 0.1),
    }


def reference(x, p, positions, pre_mix_norm, w_in, swa_sinks, rg_conv_w, rg_conv_b,
              rg_gate_a_w, rg_gate_a_b, rg_gate_x_w, rg_gate_x_b, rg_lambda,
              mla_q_norm, mla_w_uq, mla_kv_norm, mla_w_ukv, group_norm, w_out,
              post_mix_norm, pre_ffn_norm, w_gate, w_up, w_down, post_ffn_norm,
              w_ple, ple_norm, w_ple_gate, b_ple_gate):
    B, S = x.shape[0], x.shape[1]
    for i in range(DEPTH):
        h = rmsnorm(x, pre_mix_norm[i])
        z = h @ w_in[i]
        q_a, k_a, v_a, x_r, g_r, c_q, c_kv, k_r = jnp.split(z, IN_SPLITS, axis=-1)
        q_a = rope(q_a.reshape(B, S, SWA_HEADS, HEAD_DIM), positions)
        k_a = rope(k_a.reshape(B, S, SWA_KV_HEADS, HEAD_DIM), positions)
        v_a = v_a.reshape(B, S, SWA_KV_HEADS, HEAD_DIM)
        o_a = swa_attention(q_a, k_a, v_a, swa_sinks[i])
        o_b = rglru(x_r, rg_conv_w[i], rg_conv_b[i], rg_gate_a_w[i], rg_gate_a_b[i],
                    rg_gate_x_w[i], rg_gate_x_b[i], rg_lambda[i]) * jax.nn.gelu(g_r)
        o_c = mla_attention(c_q, c_kv, k_r, positions, mla_q_norm[i], mla_w_uq[i],
                            mla_kv_norm[i], mla_w_ukv[i])
        gn = group_norm[i]
        mixed = jnp.concatenate([
            rmsnorm(o_a, gn[:SWA_OUT]),
            rmsnorm(o_b, gn[SWA_OUT:SWA_OUT + RG_WIDTH]),
            rmsnorm(o_c, gn[SWA_OUT + RG_WIDTH:]),
        ], axis=-1)
        x = x + rmsnorm(mixed @ w_out[i], post_mix_norm[i])
        h = rmsnorm(x, pre_ffn_norm[i])
        f = (jax.nn.silu(h @ w_gate[i]) * (h @ w_up[i])) @ w_down[i]
        x = x + rmsnorm(f, post_ffn_norm[i])
        e = rmsnorm(p[i] @ w_ple[i], ple_norm[i])
        x = x + jax.nn.sigmoid(x @ w_ple_gate[i] + b_ple_gate[i]) * e
    return x
```

```python
import functools

import jax
import jax.numpy as jnp
import numpy as np
from jax import lax
from jax.experimental import pallas as pl
from jax.experimental.pallas import tpu as pltpu

F32 = jnp.float32
BF16 = jnp.bfloat16

D_MODEL = 4096
HEAD_DIM = 128
ROPE_THETA = 10000.0
NORM_EPS = 1e-6
PLE_DIM = 256
NEG_INF = -1e30

SWA_HEADS = 12
SWA_KV_HEADS = 4
SWA_GROUP = SWA_HEADS // SWA_KV_HEADS
SWA_BLOCK = 128
SWA_OUT = SWA_HEADS * HEAD_DIM
SWA_KV_OUT = SWA_KV_HEADS * HEAD_DIM

RG_WIDTH = 1024
RG_BLOCKS = 8
RG_BLOCK_DIM = RG_WIDTH // RG_BLOCKS
RG_CONV = 4
RG_C = 8.0

MLA_HEADS = 12
MLA_Q_RANK = 1024
MLA_KV_RANK = 512
MLA_NOPE = 128
MLA_ROPE = 64
MLA_V = 128
MLA_OUT = MLA_HEADS * MLA_V
MLA_QK_PAD = 256

MIX_WIDTH = SWA_OUT + RG_WIDTH + MLA_OUT
Z_MAIN = SWA_OUT + 2 * SWA_KV_OUT + 2 * RG_WIDTH + MLA_Q_RANK + MLA_KV_RANK

V7X_VMEM_LIMIT_BYTES = 56 * 1024 * 1024
LANE = 128
SUBLANE = 8


def _params(*semantics):
    return pltpu.CompilerParams(dimension_semantics=semantics,
                                vmem_limit_bytes=V7X_VMEM_LIMIT_BYTES)


def _rms(xf, g):
    ms = jnp.mean(xf * xf, axis=-1, keepdims=True)
    return xf * lax.rsqrt(ms + NORM_EPS) * g


def _rope_table_kernel(pos_ref, inv_ref, sign_ref, cos_ref, sin_ref):
    ang = pos_ref[...] * inv_ref[...]
    cos_ref[...] = jnp.cos(ang)
    sin_ref[...] = jnp.sin(ang) * sign_ref[...]


def _rope_tables(pos_col, rot_dim):
    t = pos_col.shape[0]
    half = rot_dim // 2
    inv = ROPE_THETA ** (-jnp.arange(0, rot_dim, 2, dtype=F32) / rot_dim)
    inv_row = jnp.tile(inv, LANE // half)[None, :]
    sign_row = jnp.tile(jnp.concatenate([-jnp.ones((half,), F32), jnp.ones((half,), F32)]),
                        LANE // rot_dim)[None, :]
    tm = 1024
    row = pl.BlockSpec((1, LANE), lambda i: (0, 0))
    return pl.pallas_call(
        _rope_table_kernel,
        grid=(t // tm,),
        in_specs=[pl.BlockSpec((tm, 1), lambda i: (i, 0)), row, row],
        out_specs=[pl.BlockSpec((tm, LANE), lambda i: (i, 0))] * 2,
        out_shape=[jax.ShapeDtypeStruct((t, LANE), F32)] * 2,
        compiler_params=_params("parallel"),
        name="rope_tables",
    )(pos_col, inv_row, sign_row)


def _norm_cast_kernel(x_ref, g_ref, o_ref):
    o_ref[...] = _rms(x_ref[...], g_ref[...]).astype(o_ref.dtype)


def _norm_cast(x, g):
    t, d = x.shape
    tm = 256
    return pl.pallas_call(
        _norm_cast_kernel,
        grid=(t // tm,),
        in_specs=[pl.BlockSpec((tm, d), lambda i: (i, 0)),
                  pl.BlockSpec((1, d), lambda i: (0, 0))],
        out_specs=pl.BlockSpec((tm, d), lambda i: (i, 0)),
        out_shape=jax.ShapeDtypeStruct((t, d), BF16),
        compiler_params=_params("parallel"),
        name="norm_cast",
    )(x, g[None, :])


def _mm_kernel(*refs, k_sizes):
    n_in = len(k_sizes)
    w_ref, o_ref = refs[n_in], refs[n_in + 1]
    acc = None
    off = 0
    for x_ref, ks in zip(refs[:n_in], k_sizes):
        part = jnp.dot(x_ref[...], w_ref[off:off + ks, :], preferred_element_type=F32)
        acc = part if acc is None else acc + part
        off += ks
    o_ref[...] = acc.astype(o_ref.dtype)


def _matmul(xs, w, tm, tn, out_dtype=F32, name="matmul"):
    m = xs[0].shape[0]
    k_sizes = tuple(x.shape[1] for x in xs)
    k, n = w.shape
    assert sum(k_sizes) == k and m % tm == 0 and n % tn == 0
    in_specs = [pl.BlockSpec((tm, ks), lambda i, j: (i, 0)) for ks in k_sizes]
    in_specs.append(pl.BlockSpec((k, tn), lambda i, j: (0, j)))
    return pl.pallas_call(
        functools.partial(_mm_kernel, k_sizes=k_sizes),
        grid=(m // tm, n // tn),
        in_specs=in_specs,
        out_specs=pl.BlockSpec((tm, tn), lambda i, j: (i, j)),
        out_shape=jax.ShapeDtypeStruct((m, n), out_dtype),
        compiler_params=_params("parallel", "arbitrary"),
        name=name,
    )(*xs, w)


def _ffn_up_kernel(x_ref, wg_ref, wu_ref, o_ref):
    x = x_ref[...]
    g = jnp.dot(x, wg_ref[...], preferred_element_type=F32)
    u = jnp.dot(x, wu_ref[...], preferred_element_type=F32)
    o_ref[...] = (g * jax.nn.sigmoid(g) * u).astype(o_ref.dtype)


def _ffn_up(h, wg, wu, tm, tn):
    m, k = h.shape
    n = wg.shape[1]
    assert m % tm == 0 and n % tn == 0
    wspec = pl.BlockSpec((k, tn), lambda i, j: (0, j))
    return pl.pallas_call(
        _ffn_up_kernel,
        grid=(m // tm, n // tn),
        in_specs=[pl.BlockSpec((tm, k), lambda i, j: (i, 0)), wspec, wspec],
        out_specs=pl.BlockSpec((tm, tn), lambda i, j: (i, j)),
        out_shape=jax.ShapeDtypeStruct((m, n), BF16),
        compiler_params=_params("parallel", "arbitrary"),
        name="ffn_up",
    )(h, wg, wu)


def _mm_ksplit_kernel(x_ref, w_ref, o_ref):
    part = jnp.dot(x_ref[...], w_ref[...], preferred_element_type=F32)

    @pl.when(pl.program_id(2) == 0)
    def _():
        o_ref[...] = part

    @pl.when(pl.program_id(2) != 0)
    def _():
        o_ref[...] += part


def _matmul_ksplit(x, w, tm, tn, tk):
    m, k = x.shape
    n = w.shape[1]
    assert m % tm == 0 and n % tn == 0 and k % tk == 0
    return pl.pallas_call(
        _mm_ksplit_kernel,
        grid=(m // tm, n // tn, k // tk),
        in_specs=[pl.BlockSpec((tm, tk), lambda i, j, kk: (i, kk)),
                  pl.BlockSpec((tk, tn), lambda i, j, kk: (kk, j))],
        out_specs=pl.BlockSpec((tm, tn), lambda i, j, kk: (i, j)),
        out_shape=jax.ShapeDtypeStruct((m, n), F32),
        compiler_params=_params("parallel", "arbitrary", "arbitrary"),
        name="ffn_down",
    )(x, w)


def _resid_norm_kernel(x_ref, y_ref, gp_ref, gn_ref, xo_ref, ho_ref, *, norm_next):
    xn = x_ref[...] + _rms(y_ref[...], gp_ref[...])
    xo_ref[...] = xn
    if norm_next:
        ho_ref[...] = _rms(xn, gn_ref[...]).astype(ho_ref.dtype)
    else:
        ho_ref[...] = xn.astype(ho_ref.dtype)


def _resid_norm(x, y, g_post, g_next):
    t, d = x.shape
    tm = 256
    norm_next = g_next is not None
    if g_next is None:
        g_next = g_post
    big = pl.BlockSpec((tm, d), lambda i: (i, 0))
    row = pl.BlockSpec((1, d), lambda i: (0, 0))
    return pl.pallas_call(
        functools.partial(_resid_norm_kernel, norm_next=norm_next),
        grid=(t // tm,),
        in_specs=[big, big, row, row],
        out_specs=[big, big],
        out_shape=[jax.ShapeDtypeStruct((t, d), F32), jax.ShapeDtypeStruct((t, d), BF16)],
        compiler_params=_params("parallel"),
        name="resid_norm",
    )(x, y, g_post[None, :], g_next[None, :])


def _ple_kernel(x_ref, gate_ref, p_ref, wp_ref, pn_ref, b_ref, gn_ref, xo_ref, ho_ref, *, norm_next):
    e = _rms(jnp.dot(p_ref[...].astype(BF16), wp_ref[...], preferred_element_type=F32), pn_ref[...])
    xn = x_ref[...] + jax.nn.sigmoid(gate_ref[...] + b_ref[...]) * e
    xo_ref[...] = xn
    if norm_next:
        ho_ref[...] = _rms(xn, gn_ref[...]).astype(ho_ref.dtype)
    else:
        ho_ref[...] = xn.astype(ho_ref.dtype)


def _ple(x, gate, p, w_ple, ple_norm, b_gate, g_next):
    t, d = x.shape
    tm = 256
    norm_next = g_next is not None
    if g_next is None:
        g_next = ple_norm
    big = pl.BlockSpec((tm, d), lambda i: (i, 0))
    row = pl.BlockSpec((1, d), lambda i: (0, 0))
    return pl.pallas_call(
        functools.partial(_ple_kernel, norm_next=norm_next),
        grid=(t // tm,),
        in_specs=[big, big, pl.BlockSpec((tm, PLE_DIM), lambda i: (i, 0)),
                  pl.BlockSpec((PLE_DIM, d), lambda i: (0, 0)), row, row, row],
        out_specs=[big, big],
        out_shape=[jax.ShapeDtypeStruct((t, d), F32), jax.ShapeDtypeStruct((t, d), BF16)],
        compiler_params=_params("parallel"),
        name="ple",
    )(x, gate, p, w_ple, ple_norm[None, :], b_gate[None, :], g_next[None, :])


def _swa_kernel(sink_ref, q_ref, kc_ref, kp_ref, vc_ref, vp_ref,
                cosc_ref, sinc_ref, cosp_ref, sinp_ref, gn_ref, o_ref):
    blk = SWA_BLOCK
    n = pl.program_id(1)
    cos_c, sin_c = cosc_ref[...], sinc_ref[...]
    cos_p, sin_p = cosp_ref[...], sinp_ref[...]

    def rope(x, c, s):
        return x * c + pltpu.roll(x, HEAD_DIM // 2, axis=1) * s

    row = lax.broadcasted_iota(jnp.int32, (blk, 2 * blk), 0)
    col = lax.broadcasted_iota(jnp.int32, (blk, 2 * blk), 1)
    dist = blk + row - col
    kpos = (n - 1) * blk + col
    valid = (dist >= 0) & (dist < blk) & (kpos >= 0)
    scale = HEAD_DIM ** -0.5

    outs = []
    sumsq = jnp.zeros((blk, 1), F32)
    for kv in range(SWA_KV_HEADS):
        ks = slice(kv * HEAD_DIM, (kv + 1) * HEAD_DIM)
        k = jnp.concatenate([rope(kp_ref[:, ks], cos_p, sin_p),
                             rope(kc_ref[:, ks], cos_c, sin_c)], axis=0).astype(BF16)
        v = jnp.concatenate([vp_ref[:, ks], vc_ref[:, ks]], axis=0).astype(BF16)
        for g in range(SWA_GROUP):
            h = kv * SWA_GROUP + g
            q = rope(q_ref[:, h * HEAD_DIM:(h + 1) * HEAD_DIM], cos_c, sin_c).astype(BF16)
            s = lax.dot_general(q, k, (((1,), (1,)), ((), ())), preferred_element_type=F32) * scale
            s = jnp.where(valid, s, NEG_INF)
            sink = sink_ref[h]
            m = jnp.maximum(jnp.max(s, axis=-1, keepdims=True), sink)
            p = jnp.exp(s - m)
            denom = jnp.sum(p, axis=-1, keepdims=True) + jnp.exp(sink - m)
            o = jnp.dot(p.astype(BF16), v, preferred_element_type=F32) / denom
            outs.append(o)
            sumsq = sumsq + jnp.sum(o * o, axis=-1, keepdims=True)
    inv = lax.rsqrt(sumsq / SWA_OUT + NORM_EPS)
    for h, o in enumerate(outs):
        hs = slice(h * HEAD_DIM, (h + 1) * HEAD_DIM)
        o_ref[:, hs] = (o * inv * gn_ref[:, hs]).astype(o_ref.dtype)


def _swa(z3, cos, sin, sinks, gn):
    b, s, _ = z3.shape
    blk = SWA_BLOCK
    nb = s // blk
    kcol = SWA_OUT // SWA_KV_OUT
    vcol = kcol + 1
    cur = lambda bi, n: (bi, n, 0)
    prev = lambda bi, n: (bi, jnp.maximum(n - 1, 0), 0)
    tab = pl.BlockSpec((None, blk, LANE), cur)
    tab_prev = pl.BlockSpec((None, blk, LANE), prev)
    return pl.pallas_call(
        _swa_kernel,
        grid=(b, nb),
        in_specs=[
            pl.BlockSpec(memory_space=pltpu.SMEM),
            pl.BlockSpec((None, blk, SWA_OUT), cur),
            pl.BlockSpec((None, blk, SWA_KV_OUT), lambda bi, n: (bi, n, kcol)),
            pl.BlockSpec((None, blk, SWA_KV_OUT), lambda bi, n: (bi, jnp.maximum(n - 1, 0), kcol)),
            pl.BlockSpec((None, blk, SWA_KV_OUT), lambda bi, n: (bi, n, vcol)),
            pl.BlockSpec((None, blk, SWA_KV_OUT), lambda bi, n: (bi, jnp.maximum(n - 1, 0), vcol)),
            tab, tab, tab_prev, tab_prev,
            pl.BlockSpec((1, SWA_OUT), lambda bi, n: (0, 0)),
        ],
        out_specs=pl.BlockSpec((None, blk, SWA_OUT), cur),
        out_shape=jax.ShapeDtypeStruct((b, s, SWA_OUT), BF16),
        compiler_params=_params("parallel", "arbitrary"),
        name="swa",
    )(sinks, z3, z3, z3, z3, z3, cos, sin, cos, sin, gn[None, :])


RG_CHUNK = 256


def _rglru_kernel(x0_ref, x1_ref, g0_ref, g1_ref, cw_ref, cb_ref, wa_ref, ba_ref, wx_ref, bx_ref,
                  lam_ref, gn_ref, o_ref, xe_ref, a_ref, b_ref, h_ref, carry_ref):
    lc = RG_CHUNK
    pad = SUBLANE

    @pl.when(pl.program_id(1) == 0)
    def _():
        xe_ref[0:pad, :] = jnp.zeros((pad, RG_WIDTH), F32)
        carry_ref[...] = jnp.zeros_like(carry_ref)

    xe_ref[pad:pad + lc, :] = jnp.concatenate([x0_ref[...], x1_ref[...]], axis=1)
    cw = cw_ref[...]
    xc = cb_ref[...]
    for j in range(RG_CONV):
        sh = RG_CONV - 1 - j
        xc = xc + xe_ref[pad - sh:pad - sh + lc, :] * cw[j:j + 1, :]
    xe_ref[0:pad, :] = xe_ref[lc:lc + pad, :]

    xcb = xc.astype(BF16)

    def gate(w_ref, bias_ref):
        parts = [jnp.dot(xcb[:, nb * RG_BLOCK_DIM:(nb + 1) * RG_BLOCK_DIM], w_ref[nb],
                         preferred_element_type=F32) for nb in range(RG_BLOCKS)]
        return jax.nn.sigmoid(jnp.concatenate(parts, axis=1) + bias_ref[...])

    r = gate(wa_ref, ba_ref)
    i = gate(wx_ref, bx_ref)
    nlam = -lam_ref[...]
    softplus = jnp.maximum(nlam, 0.0) + jnp.log1p(jnp.exp(-jnp.abs(nlam)))
    log_a = (-RG_C * softplus) * r
    a = jnp.exp(log_a)
    one_minus_a2 = -jnp.tanh(log_a) * (a * a + 1.0)
    a_ref[...] = a
    b_ref[...] = jnp.sqrt(one_minus_a2) * i * xc

    sub = lax.broadcasted_iota(jnp.int32, (SUBLANE, RG_WIDTH), 0)

    def scan_tile(t, h):
        r0 = pl.multiple_of(t * SUBLANE, SUBLANE)
        at = a_ref[pl.ds(r0, SUBLANE), :]
        bt = b_ref[pl.ds(r0, SUBLANE), :]
        for sh in (1, 2, 4):
            keep = sub >= sh
            a_sh = pltpu.roll(at, sh, axis=0)
            b_sh = pltpu.roll(bt, sh, axis=0)
            bt = jnp.where(keep, at * b_sh + bt, bt)
            at = jnp.where(keep, at * a_sh, at)
        ht = at * h + bt
        h_ref[pl.ds(r0, SUBLANE), :] = ht
        return ht[SUBLANE - 1:SUBLANE, :]

    carry_ref[0:1, :] = lax.fori_loop(0, lc // SUBLANE, scan_tile, carry_ref[0:1, :])

    g = jnp.concatenate([g0_ref[...], g1_ref[...]], axis=1)
    y = h_ref[...] * jax.nn.gelu(g)
    o_ref[...] = _rms(y, gn_ref[...]).astype(o_ref.dtype)


def _rglru(z3, conv_w, conv_b, wa, ba, wx, bx, lam, gn):
    b, s, _ = z3.shape
    lc = RG_CHUNK
    half = RG_WIDTH // 2
    x_col = (SWA_OUT + 2 * SWA_KV_OUT) // half
    g_col = x_col + 2

    def zcol(c):
        return pl.BlockSpec((None, lc, half), lambda bi, ci: (bi, ci, c))

    row = pl.BlockSpec((1, RG_WIDTH), lambda bi, ci: (0, 0))
    wspec = pl.BlockSpec((RG_BLOCKS, RG_BLOCK_DIM, RG_BLOCK_DIM), lambda bi, ci: (0, 0, 0))
    return pl.pallas_call(
        _rglru_kernel,
        grid=(b, s // lc),
        in_specs=[zcol(x_col), zcol(x_col + 1), zcol(g_col), zcol(g_col + 1),
                  pl.BlockSpec((RG_CONV, RG_WIDTH), lambda bi, ci: (0, 0)), row,
                  wspec, row, wspec, row, row, row],
        out_specs=pl.BlockSpec((None, lc, RG_WIDTH), lambda bi, ci: (bi, ci, 0)),
        out_shape=jax.ShapeDtypeStruct((b, s, RG_WIDTH), BF16),
        scratch_shapes=[pltpu.VMEM((lc + SUBLANE, RG_WIDTH), F32),
                        pltpu.VMEM((lc, RG_WIDTH), F32),
                        pltpu.VMEM((lc, RG_WIDTH), F32),
                        pltpu.VMEM((lc, RG_WIDTH), F32),
                        pltpu.VMEM((SUBLANE, RG_WIDTH), F32)],
        compiler_params=_params("parallel", "arbitrary"),
        name="rglru",
    )(z3, z3, z3, z3, conv_w, conv_b[None, :], wa, ba[None, :], wx, bx[None, :],
      lam[None, :], gn[None, :])


def _rope64(x, cos, sin, first_half):
    swapped = jnp.where(first_half, pltpu.roll(x, LANE - MLA_ROPE // 2, axis=1),
                        pltpu.roll(x, MLA_ROPE // 2, axis=1))
    return x * cos + swapped * sin


def _mla_proj_kernel(cq0_ref, cq1_ref, ckv_ref, kr_ref, cos_ref, sin_ref, qn_ref, kvn_ref,
                     wqn_ref, wqr_ref, wkn_ref, wv_ref, q_out, kn_out, kr_out, v_out):
    cos, sin = cos_ref[...], sin_ref[...]
    tm = cos.shape[0]
    lane = lax.broadcasted_iota(jnp.int32, (tm, LANE), 1)
    first_half = (lane % MLA_ROPE) < (MLA_ROPE // 2)
    low = lane < MLA_ROPE

    hq = _rms(jnp.concatenate([cq0_ref[...], cq1_ref[...]], axis=1), qn_ref[...]).astype(BF16)
    qn = jnp.dot(hq, wqn_ref[...], preferred_element_type=F32)
    qr = jnp.dot(hq, wqr_ref[...], preferred_element_type=F32)
    for pair in range(MLA_HEADS // 2):
        rot = _rope64(qr[:, pair * LANE:(pair + 1) * LANE], cos, sin, first_half)
        for odd in range(2):
            h = 2 * pair + odd
            piece = pltpu.roll(rot, MLA_ROPE, axis=1) if odd else rot
            q_out[h, :, 0:MLA_NOPE] = qn[:, h * MLA_NOPE:(h + 1) * MLA_NOPE].astype(BF16)
            q_out[h, :, MLA_NOPE:MLA_QK_PAD] = jnp.where(low, piece, 0.0).astype(BF16)

    hkv = _rms(ckv_ref[...], kvn_ref[...]).astype(BF16)
    kn = jnp.dot(hkv, wkn_ref[...], preferred_element_type=F32)
    v = jnp.dot(hkv, wv_ref[...], preferred_element_type=F32)
    for h in range(MLA_HEADS):
        hs = slice(h * LANE, (h + 1) * LANE)
        kn_out[h] = kn[:, hs].astype(BF16)
        v_out[h] = v[:, hs].astype(BF16)
    kr_out[...] = _rope64(kr_ref[...], cos, sin, first_half).astype(BF16)


def _mla_proj(z, kr, cos64, sin64, q_norm, kv_norm, wqn, wqr, wkn, wv, b, s):
    t = z.shape[0]
    tm = 256
    per_b = s // tm
    half = MLA_Q_RANK // 2
    cq_col = (SWA_OUT + 2 * SWA_KV_OUT + 2 * RG_WIDTH) // half
    tok = lambda i: (i, 0)
    head_blk = lambda i: (i // per_b, 0, i % per_b, 0)
    full = lambda shape: pl.BlockSpec(shape, lambda i: (0, 0))
    return pl.pallas_call(
        _mla_proj_kernel,
        grid=(t // tm,),
        in_specs=[pl.BlockSpec((tm, half), lambda i: (i, cq_col)),
                  pl.BlockSpec((tm, half), lambda i: (i, cq_col + 1)),
                  pl.BlockSpec((tm, MLA_KV_RANK), lambda i: (i, cq_col + 2)),
                  pl.BlockSpec((tm, LANE), tok), pl.BlockSpec((tm, LANE), tok),
                  pl.BlockSpec((tm, LANE), tok),
                  full((1, MLA_Q_RANK)), full((1, MLA_KV_RANK)),
                  full(wqn.shape), full(wqr.shape), full(wkn.shape), full(wv.shape)],
        out_specs=[pl.BlockSpec((None, MLA_HEADS, tm, MLA_QK_PAD), head_blk),
                   pl.BlockSpec((None, MLA_HEADS, tm, LANE), head_blk),
                   pl.BlockSpec((None, tm, LANE), lambda i: (i // per_b, i % per_b, 0)),
                   pl.BlockSpec((None, MLA_HEADS, tm, LANE), head_blk)],
        out_shape=[jax.ShapeDtypeStruct((b, MLA_HEADS, s, MLA_QK_PAD), BF16),
                   jax.ShapeDtypeStruct((b, MLA_HEADS, s, LANE), BF16),
                   jax.ShapeDtypeStruct((b, s, LANE), BF16),
                   jax.ShapeDtypeStruct((b, MLA_HEADS, s, LANE), BF16)],
        compiler_params=_params("parallel"),
        name="mla_proj",
    )(z, z, z, kr, cos64, sin64, q_norm[None, :], kv_norm[None, :], wqn, wqr, wkn, wv)


MLA_TQ = 256


def _mla_attn_kernel(q_ref, kn_ref, kr_ref, v_ref, gn_ref, o_ref, oh_ref):
    tq = MLA_TQ
    qi = pl.program_id(1)
    scale = (MLA_NOPE + MLA_ROPE) ** -0.5
    row = lax.broadcasted_iota(jnp.int32, (tq, tq), 0)
    col = lax.broadcasted_iota(jnp.int32, (tq, tq), 1)

    for h in range(MLA_HEADS):
        q = q_ref[h]

        def kv_step(j, carry, h=h, q=q):
            m, l, acc = carry
            off = pl.multiple_of(j * tq, tq)
            k = jnp.concatenate([kn_ref[h, pl.ds(off, tq), :], kr_ref[pl.ds(off, tq), :]], axis=1)
            s = lax.dot_general(q, k, (((1,), (1,)), ((), ())), preferred_element_type=F32) * scale
            s = jnp.where(col + (j - qi) * tq <= row, s, NEG_INF)
            m_new = jnp.maximum(m, jnp.max(s, axis=-1, keepdims=True))
            alpha = jnp.exp(m - m_new)
            p = jnp.exp(s - m_new)
            l = alpha * l + jnp.sum(p, axis=-1, keepdims=True)
            acc = alpha * acc + jnp.dot(p.astype(BF16), v_ref[h, pl.ds(off, tq), :],
                                        preferred_element_type=F32)
            return m_new, l, acc

        init = (jnp.full((tq, 1), NEG_INF, F32), jnp.zeros((tq, 1), F32), jnp.zeros((tq, MLA_V), F32))
        _, l, acc = lax.fori_loop(0, qi + 1, kv_step, init)
        oh_ref[h] = acc / l

    sumsq = jnp.zeros((tq, 1), F32)
    for h in range(MLA_HEADS):
        o = oh_ref[h]
        sumsq = sumsq + jnp.sum(o * o, axis=-1, keepdims=True)
    inv = lax.rsqrt(sumsq / MLA_OUT + NORM_EPS)
    for h in range(MLA_HEADS):
        hs = slice(h * MLA_V, (h + 1) * MLA_V)
        o_ref[:, hs] = (oh_ref[h] * inv * gn_ref[:, hs]).astype(o_ref.dtype)


def _mla_attn(q, kn, kr, v, gn):
    b, _, s, _ = q.shape
    tq = MLA_TQ
    return pl.pallas_call(
        _mla_attn_kernel,
        grid=(b, s // tq),
        in_specs=[pl.BlockSpec((None, MLA_HEADS, tq, MLA_QK_PAD), lambda bi, qi: (bi, 0, qi, 0)),
                  pl.BlockSpec((None, MLA_HEADS, s, LANE), lambda bi, qi: (bi, 0, 0, 0)),
                  pl.BlockSpec((None, s, LANE), lambda bi, qi: (bi, 0, 0)),
                  pl.BlockSpec((None, MLA_HEADS, s, LANE), lambda bi, qi: (bi, 0, 0, 0)),
                  pl.BlockSpec((1, MLA_OUT), lambda bi, qi: (0, 0))],
        out_specs=pl.BlockSpec((None, tq, MLA_OUT), lambda bi, qi: (bi, qi, 0)),
        out_shape=jax.ShapeDtypeStruct((b, s, MLA_OUT), BF16),
        scratch_shapes=[pltpu.VMEM((MLA_HEADS, tq, MLA_V), F32)],
        compiler_params=_params("parallel", "arbitrary"),
        name="mla_attn",
    )(q, kn, kr, v, gn[None, :])


def kernel(x, p, positions, pre_mix_norm, w_in, swa_sinks, rg_conv_w, rg_conv_b, rg_gate_a_w,
           rg_gate_a_b, rg_gate_x_w, rg_gate_x_b, rg_lambda, mla_q_norm, mla_w_uq, mla_kv_norm,
           mla_w_ukv, group_norm, w_out, post_mix_norm, pre_ffn_norm, w_gate, w_up, w_down,
           post_ffn_norm, w_ple, ple_norm, w_ple_gate, b_ple_gate):
    b, s, d = x.shape
    t = b * s
    depth = w_in.shape[0]
    d_ff = w_gate.shape[-1]

    pos_col = positions.astype(F32).reshape(t, 1)
    cos128, sin128 = _rope_tables(pos_col, HEAD_DIM)
    cos64, sin64 = _rope_tables(pos_col, MLA_ROPE)
    cos128_3, sin128_3 = cos128.reshape(b, s, LANE), sin128.reshape(b, s, LANE)

    xf = x.reshape(t, d)
    h = _norm_cast(xf, pre_mix_norm[0])
    for i in range(depth):
        w_in_b = w_in[i].astype(BF16)
        z = _matmul([h], w_in_b[:, :Z_MAIN], 1024, 1024, name="in_proj")
        w_kr = jnp.pad(w_in_b[:, Z_MAIN:], ((0, 0), (0, LANE - MLA_ROPE)))
        kr = _matmul([h], w_kr, 1024, LANE, name="in_proj_kr")
        z3 = z.reshape(b, s, Z_MAIN)
        gn = group_norm[i]

        o_a = _swa(z3, cos128_3, sin128_3, swa_sinks[i], gn[:SWA_OUT])
        o_b = _rglru(z3, rg_conv_w[i], rg_conv_b[i], rg_gate_a_w[i].astype(BF16), rg_gate_a_b[i],
                     rg_gate_x_w[i].astype(BF16), rg_gate_x_b[i], rg_lambda[i],
                     gn[SWA_OUT:SWA_OUT + RG_WIDTH])
        w_uq = mla_w_uq[i].astype(BF16).reshape(MLA_Q_RANK, MLA_HEADS, MLA_NOPE + MLA_ROPE)
        w_ukv = mla_w_ukv[i].astype(BF16).reshape(MLA_KV_RANK, MLA_HEADS, MLA_NOPE + MLA_V)
        q_c, kn_c, kr_c, v_c = _mla_proj(
            z, kr, cos64, sin64, mla_q_norm[i], mla_kv_norm[i],
            w_uq[:, :, :MLA_NOPE].reshape(MLA_Q_RANK, MLA_HEADS * MLA_NOPE),
            w_uq[:, :, MLA_NOPE:].reshape(MLA_Q_RANK, MLA_HEADS * MLA_ROPE),
            w_ukv[:, :, :MLA_NOPE].reshape(MLA_KV_RANK, MLA_HEADS * MLA_NOPE),
            w_ukv[:, :, MLA_NOPE:].reshape(MLA_KV_RANK, MLA_HEADS * MLA_V), b, s)
        o_c = _mla_attn(q_c, kn_c, kr_c, v_c, gn[SWA_OUT + RG_WIDTH:])

        mix = _matmul([o_a.reshape(t, SWA_OUT), o_b.reshape(t, RG_WIDTH), o_c.reshape(t, MLA_OUT)],
                      w_out[i].astype(BF16), 1024, 1024, name="out_proj")
        xf, h = _resid_norm(xf, mix, post_mix_norm[i], pre_ffn_norm[i])

        gu = _ffn_up(h, w_gate[i].astype(BF16), w_up[i].astype(BF16), 1024, 256)
        f = _matmul_ksplit(gu, w_down[i].astype(BF16), 512, 1024, d_ff // 2)
        xf, xb = _resid_norm(xf, f, post_ffn_norm[i], None)

        gate = _matmul([xb], w_ple_gate[i].astype(BF16), 1024, 1024, name="ple_gate")
        g_next = pre_mix_norm[i + 1] if i + 1 < depth else None
        xf, h = _ple(xf, gate, p[i].reshape(t, PLE_DIM), w_ple[i].astype(BF16), ple_norm[i],
                     b_ple_gate[i], g_next)
    return xf.reshape(b, s, d)
```

```python
import functools
import math

import jax
import jax.numpy as jnp
import numpy as np
from jax import lax
from jax.experimental import pallas as pl
from jax.experimental.pallas import tpu as pltpu

F32 = jnp.float32
BF16 = jnp.bfloat16

D_MODEL = 4096
HEAD_DIM = 128
ROPE_THETA = 10000.0
NORM_EPS = 1e-6
PLE_DIM = 256
NEG_INF = -1e30

SWA_HEADS = 12
SWA_KV_HEADS = 4
SWA_GROUP = SWA_HEADS // SWA_KV_HEADS
SWA_BLOCK = 128
SWA_OUT = SWA_HEADS * HEAD_DIM
SWA_KV_OUT = SWA_KV_HEADS * HEAD_DIM

RG_WIDTH = 1024
RG_BLOCKS = 8
RG_BLOCK_DIM = RG_WIDTH // RG_BLOCKS
RG_CONV = 4
RG_C = 8.0

MLA_HEADS = 12
MLA_Q_RANK = 1024
MLA_KV_RANK = 512
MLA_NOPE = 128
MLA_ROPE = 64
MLA_V = 128
MLA_OUT = MLA_HEADS * MLA_V
MLA_QK_PAD = 256
MLA_Q_SCALE = (MLA_NOPE + MLA_ROPE) ** -0.5 * math.log2(math.e)

MIX_WIDTH = SWA_OUT + RG_WIDTH + MLA_OUT
Z_MAIN = SWA_OUT + 2 * SWA_KV_OUT + 2 * RG_WIDTH + MLA_Q_RANK + MLA_KV_RANK

V7X_VMEM_LIMIT_BYTES = 56 * 1024 * 1024
LANE = 128
SUBLANE = 8


def _params(*semantics):
    return pltpu.CompilerParams(dimension_semantics=semantics,
                                vmem_limit_bytes=V7X_VMEM_LIMIT_BYTES)


def _rms(xf, g):
    ms = jnp.mean(xf * xf, axis=-1, keepdims=True)
    return xf * lax.rsqrt(ms + NORM_EPS) * g


def _rope_table_kernel(pos_ref, inv_ref, sign_ref, cos_ref, sin_ref):
    ang = pos_ref[...] * inv_ref[...]
    cos_ref[...] = jnp.cos(ang)
    sin_ref[...] = jnp.sin(ang) * sign_ref[...]


def _rope_tables(pos_col, rot_dim):
    t = pos_col.shape[0]
    half = rot_dim // 2
    inv = ROPE_THETA ** (-jnp.arange(0, rot_dim, 2, dtype=F32) / rot_dim)
    inv_row = jnp.tile(inv, LANE // half)[None, :]
    sign_row = jnp.tile(jnp.concatenate([-jnp.ones((half,), F32), jnp.ones((half,), F32)]),
                        LANE // rot_dim)[None, :]
    tm = 1024
    row = pl.BlockSpec((1, LANE), lambda i: (0, 0))
    return pl.pallas_call(
        _rope_table_kernel,
        grid=(t // tm,),
        in_specs=[pl.BlockSpec((tm, 1), lambda i: (i, 0)), row, row],
        out_specs=[pl.BlockSpec((tm, LANE), lambda i: (i, 0))] * 2,
        out_shape=[jax.ShapeDtypeStruct((t, LANE), F32)] * 2,
        compiler_params=_params("parallel"),
        name="rope_tables",
    )(pos_col, inv_row, sign_row)


def _norm_cast_kernel(x_ref, g_ref, o_ref):
    o_ref[...] = _rms(x_ref[...], g_ref[...]).astype(o_ref.dtype)


def _norm_cast(x, g):
    t, d = x.shape
    tm = 256
    return pl.pallas_call(
        _norm_cast_kernel,
        grid=(t // tm,),
        in_specs=[pl.BlockSpec((tm, d), lambda i: (i, 0)),
                  pl.BlockSpec((1, d), lambda i: (0, 0))],
        out_specs=pl.BlockSpec((tm, d), lambda i: (i, 0)),
        out_shape=jax.ShapeDtypeStruct((t, d), BF16),
        compiler_params=_params("parallel"),
        name="norm_cast",
    )(x, g[None, :])


def _mm_kernel(*refs, k_sizes):
    n_in = len(k_sizes)
    w_ref, o_ref, wb_ref = refs[n_in], refs[n_in + 1], refs[n_in + 2]

    @pl.when(pl.program_id(2) == 0)
    def _():
        wb_ref[...] = w_ref[...].astype(BF16)

    acc = None
    off = 0
    for x_ref, ks in zip(refs[:n_in], k_sizes):
        part = jnp.dot(x_ref[...], wb_ref[off:off + ks, :], preferred_element_type=F32)
        acc = part if acc is None else acc + part
        off += ks
    o_ref[...] = acc.astype(o_ref.dtype)


def _matmul(xs, w_stack, layer, tm, tn, n_cols=None, k_chunks=1, out_dtype=F32, name="matmul"):
    m = xs[0].shape[0]
    _, k, n = w_stack.shape
    n_cols = n if n_cols is None else n_cols
    k_sizes = tuple(x.shape[1] for x in xs)
    assert m % tm == 0 and n_cols % tn == 0 and k % k_chunks == 0
    tk = k // k_chunks
    if k_chunks == 1:
        assert sum(k_sizes) == k
        in_specs = [pl.BlockSpec((tm, ks), lambda j, c, i: (i, 0)) for ks in k_sizes]
        out_specs = pl.BlockSpec((tm, tn), lambda j, c, i: (i, j))
        out_shape = jax.ShapeDtypeStruct((m, n_cols), out_dtype)
    else:
        assert k_sizes == (k,)
        k_sizes = (tk,)
        in_specs = [pl.BlockSpec((tm, tk), lambda j, c, i: (i, c))]
        out_specs = pl.BlockSpec((None, tm, tn), lambda j, c, i: (c, i, j))
        out_shape = jax.ShapeDtypeStruct((k_chunks, m, n_cols), out_dtype)
    in_specs.append(pl.BlockSpec((None, tk, tn), lambda j, c, i: (layer, c, j)))
    return pl.pallas_call(
        functools.partial(_mm_kernel, k_sizes=k_sizes),
        grid=(n_cols // tn, k_chunks, m // tm),
        in_specs=in_specs,
        out_specs=out_specs,
        out_shape=out_shape,
        scratch_shapes=[pltpu.VMEM((tk, tn), BF16)],
        compiler_params=_params("arbitrary", "arbitrary", "arbitrary"),
        name=name,
    )(*xs, w_stack)


def _ffn_up_kernel(x_ref, wg_ref, wu_ref, o_ref, wgb_ref, wub_ref):
    @pl.when(pl.program_id(1) == 0)
    def _():
        wgb_ref[...] = wg_ref[...].astype(BF16)
        wub_ref[...] = wu_ref[...].astype(BF16)

    x = x_ref[...]
    g = jnp.dot(x, wgb_ref[...], preferred_element_type=F32)
    u = jnp.dot(x, wub_ref[...], preferred_element_type=F32)
    o_ref[...] = (g * jax.nn.sigmoid(g) * u).astype(o_ref.dtype)


def _ffn_up(h, wg_stack, wu_stack, layer, tm, tn):
    m, k = h.shape
    n = wg_stack.shape[-1]
    assert m % tm == 0 and n % tn == 0
    wspec = pl.BlockSpec((None, k, tn), lambda j, i: (layer, 0, j))
    return pl.pallas_call(
        _ffn_up_kernel,
        grid=(n // tn, m // tm),
        in_specs=[pl.BlockSpec((tm, k), lambda j, i: (i, 0)), wspec, wspec],
        out_specs=pl.BlockSpec((tm, tn), lambda j, i: (i, j)),
        out_shape=jax.ShapeDtypeStruct((m, n), BF16),
        scratch_shapes=[pltpu.VMEM((k, tn), BF16), pltpu.VMEM((k, tn), BF16)],
        compiler_params=_params("arbitrary", "arbitrary"),
        name="ffn_up",
    )(h, wg_stack, wu_stack)


def _resid_norm_kernel(x_ref, y_ref, gp_ref, gn_ref, xo_ref, ho_ref, *, norm_next):
    y = y_ref[0]
    for c in range(1, y_ref.shape[0]):
        y = y + y_ref[c]
    xn = x_ref[...] + _rms(y, gp_ref[...])
    xo_ref[...] = xn
    if norm_next:
        ho_ref[...] = _rms(xn, gn_ref[...]).astype(ho_ref.dtype)
    else:
        ho_ref[...] = xn.astype(ho_ref.dtype)


def _resid_norm(x, y_parts, g_post, g_next):
    t, d = x.shape
    parts = y_parts.shape[0]
    tm = 128
    norm_next = g_next is not None
    if g_next is None:
        g_next = g_post
    big = pl.BlockSpec((tm, d), lambda i: (i, 0))
    row = pl.BlockSpec((1, d), lambda i: (0, 0))
    return pl.pallas_call(
        functools.partial(_resid_norm_kernel, norm_next=norm_next),
        grid=(t // tm,),
        in_specs=[big, pl.BlockSpec((parts, tm, d), lambda i: (0, i, 0)), row, row],
        out_specs=[big, big],
        out_shape=[jax.ShapeDtypeStruct((t, d), F32), jax.ShapeDtypeStruct((t, d), BF16)],
        compiler_params=_params("parallel"),
        name="resid_norm",
    )(x, y_parts, g_post[None, :], g_next[None, :])


def _ple_kernel(x_ref, gate_ref, p_ref, wp_ref, pn_ref, b_ref, gn_ref, xo_ref, *maybe_ho_ref):
    e = _rms(jnp.dot(p_ref[...].astype(BF16), wp_ref[...], preferred_element_type=F32), pn_ref[...])
    xn = x_ref[...] + jax.nn.sigmoid(gate_ref[...] + b_ref[...]) * e
    xo_ref[...] = xn
    for ho_ref in maybe_ho_ref:
        ho_ref[...] = _rms(xn, gn_ref[...]).astype(ho_ref.dtype)


def _ple(x, gate, p, w_ple, ple_norm, b_gate, g_next):
    t, d = x.shape
    tm = 256
    norm_next = g_next is not None
    if g_next is None:
        g_next = ple_norm
    big = pl.BlockSpec((tm, d), lambda i: (i, 0))
    row = pl.BlockSpec((1, d), lambda i: (0, 0))
    out_specs = [big, big] if norm_next else [big]
    out_shape = [jax.ShapeDtypeStruct((t, d), F32)]
    if norm_next:
        out_shape.append(jax.ShapeDtypeStruct((t, d), BF16))
    outs = pl.pallas_call(
        _ple_kernel,
        grid=(t // tm,),
        in_specs=[big, big, pl.BlockSpec((tm, PLE_DIM), lambda i: (i, 0)),
                  pl.BlockSpec((PLE_DIM, d), lambda i: (0, 0)), row, row, row],
        out_specs=out_specs,
        out_shape=out_shape,
        compiler_params=_params("parallel"),
        name="ple",
    )(x, gate, p, w_ple, ple_norm[None, :], b_gate[None, :], g_next[None, :])
    return (outs[0], outs[1]) if norm_next else (outs[0], None)


def _swa_kernel(sink_ref, q_ref, kc_ref, kp_ref, vc_ref, vp_ref,
                cosc_ref, sinc_ref, cosp_ref, sinp_ref, gn_ref, o_ref):
    blk = SWA_BLOCK
    n = pl.program_id(1)
    cos_c, sin_c = cosc_ref[...], sinc_ref[...]
    cos_p, sin_p = cosp_ref[...], sinp_ref[...]

    def rope(x, c, s):
        return x * c + pltpu.roll(x, HEAD_DIM // 2, axis=1) * s

    row = lax.broadcasted_iota(jnp.int32, (blk, 2 * blk), 0)
    col = lax.broadcasted_iota(jnp.int32, (blk, 2 * blk), 1)
    dist = blk + row - col
    kpos = (n - 1) * blk + col
    valid = (dist >= 0) & (dist < blk) & (kpos >= 0)
    scale = HEAD_DIM ** -0.5

    outs = []
    sumsq = jnp.zeros((blk, 1), F32)
    for kv in range(SWA_KV_HEADS):
        ks = slice(kv * HEAD_DIM, (kv + 1) * HEAD_DIM)
        k = jnp.concatenate([rope(kp_ref[:, ks], cos_p, sin_p),
                             rope(kc_ref[:, ks], cos_c, sin_c)], axis=0).astype(BF16)
        v = jnp.concatenate([vp_ref[:, ks], vc_ref[:, ks]], axis=0).astype(BF16)
        for g in range(SWA_GROUP):
            h = kv * SWA_GROUP + g
            q = rope(q_ref[:, h * HEAD_DIM:(h + 1) * HEAD_DIM], cos_c, sin_c).astype(BF16)
            s = lax.dot_general(q, k, (((1,), (1,)), ((), ())), preferred_element_type=F32) * scale
            s = jnp.where(valid, s, NEG_INF)
            sink = sink_ref[h]
            m = jnp.maximum(jnp.max(s, axis=-1, keepdims=True), sink)
            p = jnp.exp(s - m)
            denom = jnp.sum(p, axis=-1, keepdims=True) + jnp.exp(sink - m)
            o = jnp.dot(p.astype(BF16), v, preferred_element_type=F32) / denom
            outs.append(o)
            sumsq = sumsq + jnp.sum(o * o, axis=-1, keepdims=True)
    inv = lax.rsqrt(sumsq / SWA_OUT + NORM_EPS)
    for h, o in enumerate(outs):
        hs = slice(h * HEAD_DIM, (h + 1) * HEAD_DIM)
        o_ref[:, hs] = (o * inv * gn_ref[:, hs]).astype(o_ref.dtype)


def _swa(z3, cos, sin, sinks, gn):
    b, s, _ = z3.shape
    blk = SWA_BLOCK
    nb = s // blk
    kcol = SWA_OUT // SWA_KV_OUT
    vcol = kcol + 1
    cur = lambda bi, n: (bi, n, 0)
    prev = lambda bi, n: (bi, jnp.maximum(n - 1, 0), 0)
    tab = pl.BlockSpec((None, blk, LANE), cur)
    tab_prev = pl.BlockSpec((None, blk, LANE), prev)
    return pl.pallas_call(
        _swa_kernel,
        grid=(b, nb),
        in_specs=[
            pl.BlockSpec(memory_space=pltpu.SMEM),
            pl.BlockSpec((None, blk, SWA_OUT), cur),
            pl.BlockSpec((None, blk, SWA_KV_OUT), lambda bi, n: (bi, n, kcol)),
            pl.BlockSpec((None, blk, SWA_KV_OUT), lambda bi, n: (bi, jnp.maximum(n - 1, 0), kcol)),
            pl.BlockSpec((None, blk, SWA_KV_OUT), lambda bi, n: (bi, n, vcol)),
            pl.BlockSpec((None, blk, SWA_KV_OUT), lambda bi, n: (bi, jnp.maximum(n - 1, 0), vcol)),
            tab, tab, tab_prev, tab_prev,
            pl.BlockSpec((1, SWA_OUT), lambda bi, n: (0, 0)),
        ],
        out_specs=pl.BlockSpec((None, blk, SWA_OUT), cur),
        out_shape=jax.ShapeDtypeStruct((b, s, SWA_OUT), BF16),
        compiler_params=_params("parallel", "arbitrary"),
        name="swa",
    )(sinks, z3, z3, z3, z3, z3, cos, sin, cos, sin, gn[None, :])


RG_CHUNK = 256


def _rglru_kernel(x0_ref, x1_ref, g0_ref, g1_ref, cw_ref, cb_ref, wa_ref, ba_ref, wx_ref, bx_ref,
                  lam_ref, gn_ref, o_ref, xe_ref, a_ref, b_ref, h_ref, carry_ref):
    lc = RG_CHUNK
    pad = SUBLANE

    @pl.when(pl.program_id(1) == 0)
    def _():
        xe_ref[0:pad, :] = jnp.zeros((pad, RG_WIDTH), F32)
        carry_ref[...] = jnp.zeros_like(carry_ref)

    xe_ref[pad:pad + lc, :] = jnp.concatenate([x0_ref[...], x1_ref[...]], axis=1)
    cw = cw_ref[...]
    xc = cb_ref[...]
    for j in range(RG_CONV):
        sh = RG_CONV - 1 - j
        xc = xc + xe_ref[pad - sh:pad - sh + lc, :] * cw[j:j + 1, :]
    xe_ref[0:pad, :] = xe_ref[lc:lc + pad, :]

    xcb = xc.astype(BF16)

    def gate(w_ref, bias_ref):
        parts = [jnp.dot(xcb[:, nb * RG_BLOCK_DIM:(nb + 1) * RG_BLOCK_DIM], w_ref[nb],
                         preferred_element_type=F32) for nb in range(RG_BLOCKS)]
        return jax.nn.sigmoid(jnp.concatenate(parts, axis=1) + bias_ref[...])

    r = gate(wa_ref, ba_ref)
    i = gate(wx_ref, bx_ref)
    nlam = -lam_ref[...]
    softplus = jnp.maximum(nlam, 0.0) + jnp.log1p(jnp.exp(-jnp.abs(nlam)))
    log_a = (-RG_C * softplus) * r
    a = jnp.exp(log_a)
    one_minus_a2 = -jnp.tanh(log_a) * (a * a + 1.0)
    a_ref[...] = a
    b_ref[...] = jnp.sqrt(one_minus_a2) * i * xc

    sub = lax.broadcasted_iota(jnp.int32, (SUBLANE, RG_WIDTH), 0)

    def scan_tile(t, h):
        r0 = pl.multiple_of(t * SUBLANE, SUBLANE)
        at = a_ref[pl.ds(r0, SUBLANE), :]
        bt = b_ref[pl.ds(r0, SUBLANE), :]
        for sh in (1, 2, 4):
            keep = sub >= sh
            a_sh = pltpu.roll(at, sh, axis=0)
            b_sh = pltpu.roll(bt, sh, axis=0)
            bt = jnp.where(keep, at * b_sh + bt, bt)
            at = jnp.where(keep, at * a_sh, at)
        ht = at * h + bt
        h_ref[pl.ds(r0, SUBLANE), :] = ht
        return ht[SUBLANE - 1:SUBLANE, :]

    carry_ref[0:1, :] = lax.fori_loop(0, lc // SUBLANE, scan_tile, carry_ref[0:1, :])

    g = jnp.concatenate([g0_ref[...], g1_ref[...]], axis=1)
    y = h_ref[...] * jax.nn.gelu(g)
    o_ref[...] = _rms(y, gn_ref[...]).astype(o_ref.dtype)


def _rglru(z3, conv_w, conv_b, wa, ba, wx, bx, lam, gn):
    b, s, _ = z3.shape
    lc = RG_CHUNK
    half = RG_WIDTH // 2
    x_col = (SWA_OUT + 2 * SWA_KV_OUT) // half
    g_col = x_col + 2

    def zcol(c):
        return pl.BlockSpec((None, lc, half), lambda bi, ci: (bi, ci, c))

    row = pl.BlockSpec((1, RG_WIDTH), lambda bi, ci: (0, 0))
    wspec = pl.BlockSpec((RG_BLOCKS, RG_BLOCK_DIM, RG_BLOCK_DIM), lambda bi, ci: (0, 0, 0))
    return pl.pallas_call(
        _rglru_kernel,
        grid=(b, s // lc),
        in_specs=[zcol(x_col), zcol(x_col + 1), zcol(g_col), zcol(g_col + 1),
                  pl.BlockSpec((RG_CONV, RG_WIDTH), lambda bi, ci: (0, 0)), row,
                  wspec, row, wspec, row, row, row],
        out_specs=pl.BlockSpec((None, lc, RG_WIDTH), lambda bi, ci: (bi, ci, 0)),
        out_shape=jax.ShapeDtypeStruct((b, s, RG_WIDTH), BF16),
        scratch_shapes=[pltpu.VMEM((lc + SUBLANE, RG_WIDTH), F32),
                        pltpu.VMEM((lc, RG_WIDTH), F32),
                        pltpu.VMEM((lc, RG_WIDTH), F32),
                        pltpu.VMEM((lc, RG_WIDTH), F32),
                        pltpu.VMEM((SUBLANE, RG_WIDTH), F32)],
        compiler_params=_params("parallel", "arbitrary"),
        name="rglru",
    )(z3, z3, z3, z3, conv_w, conv_b[None, :], wa, ba[None, :], wx, bx[None, :],
      lam[None, :], gn[None, :])


def _rope64(x, cos, sin, first_half):
    swapped = jnp.where(first_half, pltpu.roll(x, LANE - MLA_ROPE // 2, axis=1),
                        pltpu.roll(x, MLA_ROPE // 2, axis=1))
    return x * cos + swapped * sin


def _mla_proj_kernel(cq0_ref, cq1_ref, ckv_ref, kr_ref, cos_ref, sin_ref, qn_ref, kvn_ref,
                     wqn_ref, wqr_ref, wkn_ref, wv_ref, q_out, kn_out, kr_out, v_out):
    cos, sin = cos_ref[...], sin_ref[...]
    tm = cos.shape[0]
    lane = lax.broadcasted_iota(jnp.int32, (tm, LANE), 1)
    first_half = (lane % MLA_ROPE) < (MLA_ROPE // 2)
    low = lane < MLA_ROPE

    hq = _rms(jnp.concatenate([cq0_ref[...], cq1_ref[...]], axis=1), qn_ref[...]).astype(BF16)
    qn = jnp.dot(hq, wqn_ref[...], preferred_element_type=F32) * MLA_Q_SCALE
    qr = jnp.dot(hq, wqr_ref[...], preferred_element_type=F32) * MLA_Q_SCALE
    for pair in range(MLA_HEADS // 2):
        rot = _rope64(qr[:, pair * LANE:(pair + 1) * LANE], cos, sin, first_half)
        for odd in range(2):
            h = 2 * pair + odd
            piece = pltpu.roll(rot, MLA_ROPE, axis=1) if odd else rot
            q_out[h, :, 0:MLA_NOPE] = qn[:, h * MLA_NOPE:(h + 1) * MLA_NOPE].astype(BF16)
            q_out[h, :, MLA_NOPE:MLA_QK_PAD] = jnp.where(low, piece, 0.0).astype(BF16)

    hkv = _rms(ckv_ref[...], kvn_ref[...]).astype(BF16)
    kn = jnp.dot(hkv, wkn_ref[...], preferred_element_type=F32)
    v = jnp.dot(hkv, wv_ref[...], preferred_element_type=F32)
    for h in range(MLA_HEADS):
        hs = slice(h * LANE, (h + 1) * LANE)
        kn_out[h] = kn[:, hs].astype(BF16)
        v_out[h] = v[:, hs].astype(BF16)
    kr_out[...] = _rope64(kr_ref[...], cos, sin, first_half).astype(BF16)


def _mla_proj(z, kr, cos64, sin64, q_norm, kv_norm, wqn, wqr, wkn, wv, b, s):
    t = z.shape[0]
    tm = 256
    per_b = s // tm
    half = MLA_Q_RANK // 2
    cq_col = (SWA_OUT + 2 * SWA_KV_OUT + 2 * RG_WIDTH) // half
    tok = lambda i: (i, 0)
    head_blk = lambda i: (i // per_b, 0, i % per_b, 0)
    full = lambda shape: pl.BlockSpec(shape, lambda i: (0, 0))
    return pl.pallas_call(
        _mla_proj_kernel,
        grid=(t // tm,),
        in_specs=[pl.BlockSpec((tm, half), lambda i: (i, cq_col)),
                  pl.BlockSpec((tm, half), lambda i: (i, cq_col + 1)),
                  pl.BlockSpec((tm, MLA_KV_RANK), lambda i: (i, cq_col + 2)),
                  pl.BlockSpec((tm, LANE), tok), pl.BlockSpec((tm, LANE), tok),
                  pl.BlockSpec((tm, LANE), tok),
                  full((1, MLA_Q_RANK)), full((1, MLA_KV_RANK)),
                  full(wqn.shape), full(wqr.shape), full(wkn.shape), full(wv.shape)],
        out_specs=[pl.BlockSpec((None, MLA_HEADS, tm, MLA_QK_PAD), head_blk),
                   pl.BlockSpec((None, MLA_HEADS, tm, LANE), head_blk),
                   pl.BlockSpec((None, tm, LANE), lambda i: (i // per_b, i % per_b, 0)),
                   pl.BlockSpec((None, MLA_HEADS, tm, LANE), head_blk)],
        out_shape=[jax.ShapeDtypeStruct((b, MLA_HEADS, s, MLA_QK_PAD), BF16),
                   jax.ShapeDtypeStruct((b, MLA_HEADS, s, LANE), BF16),
                   jax.ShapeDtypeStruct((b, s, LANE), BF16),
                   jax.ShapeDtypeStruct((b, MLA_HEADS, s, LANE), BF16)],
        compiler_params=_params("parallel"),
        name="mla_proj",
    )(z, z, z, kr, cos64, sin64, q_norm[None, :], kv_norm[None, :], wqn, wqr, wkn, wv)


MLA_TQ = 256


def _mla_attn_kernel(qi_tab, kj_tab, q_ref, kn_ref, kr_ref, v_ref, gn_ref, o_ref,
                     m_ref, l_ref, acc_ref):
    tq = MLA_TQ
    step = pl.program_id(1)
    qi = qi_tab[step]
    kj = kj_tab[step]

    @pl.when(kj == 0)
    def _():
        m_ref[...] = jnp.full(m_ref.shape, NEG_INF, F32)
        l_ref[...] = jnp.zeros_like(l_ref)
        acc_ref[...] = jnp.zeros_like(acc_ref)

    row = lax.broadcasted_iota(jnp.int32, (tq, tq), 0)
    col = lax.broadcasted_iota(jnp.int32, (tq, tq), 1)
    causal = col + (kj - qi) * tq <= row
    kr = kr_ref[...]
    for h in range(MLA_HEADS):
        k = jnp.concatenate([kn_ref[h], kr], axis=1)
        s = lax.dot_general(q_ref[h], k, (((1,), (1,)), ((), ())), preferred_element_type=F32)
        s = jnp.where(causal, s, NEG_INF)
        m_old = m_ref[h]
        m_new = jnp.maximum(m_old, jnp.max(s, axis=-1, keepdims=True))
        alpha = jnp.exp2(m_old - m_new)
        p = jnp.exp2(s - jnp.concatenate([m_new, m_new], axis=1))
        l_ref[h] = alpha * l_ref[h] + (p[:, :LANE] + p[:, LANE:])
        acc_ref[h] = alpha * acc_ref[h] + jnp.dot(p.astype(BF16), v_ref[h],
                                                  preferred_element_type=F32)
        m_ref[h] = m_new

    @pl.when(kj == qi)
    def _():
        sq = jnp.zeros((tq, MLA_V), F32)
        for h in range(MLA_HEADS):
            o = acc_ref[h] / jnp.sum(l_ref[h], axis=-1, keepdims=True)
            acc_ref[h] = o
            sq = sq + o * o
        inv = lax.rsqrt(jnp.sum(sq, axis=-1, keepdims=True) / MLA_OUT + NORM_EPS)
        for h in range(MLA_HEADS):
            hs = slice(h * MLA_V, (h + 1) * MLA_V)
            o_ref[:, hs] = (acc_ref[h] * inv * gn_ref[:, hs]).astype(o_ref.dtype)


def _mla_attn(q, kn, kr, v, gn):
    b, _, s, _ = q.shape
    tq = MLA_TQ
    nq = s // tq
    pairs = [(qi, kj) for qi in range(nq) for kj in range(qi + 1)]
    qi_tab = jnp.asarray(np.array([pr[0] for pr in pairs], np.int32))
    kj_tab = jnp.asarray(np.array([pr[1] for pr in pairs], np.int32))
    head_rows = lambda tab_of: (lambda bi, st, qt, kt: (bi, 0, tab_of(qt, kt)[st], 0))
    by_q = lambda qt, kt: qt
    by_k = lambda qt, kt: kt
    grid_spec = pltpu.PrefetchScalarGridSpec(
        num_scalar_prefetch=2,
        grid=(b, len(pairs)),
        in_specs=[pl.BlockSpec((None, MLA_HEADS, tq, MLA_QK_PAD), head_rows(by_q)),
                  pl.BlockSpec((None, MLA_HEADS, tq, LANE), head_rows(by_k)),
                  pl.BlockSpec((None, tq, LANE), lambda bi, st, qt, kt: (bi, kt[st], 0)),
                  pl.BlockSpec((None, MLA_HEADS, tq, LANE), head_rows(by_k)),
                  pl.BlockSpec((1, MLA_OUT), lambda bi, st, qt, kt: (0, 0))],
        out_specs=pl.BlockSpec((None, tq, MLA_OUT), lambda bi, st, qt, kt: (bi, qt[st], 0)),
        scratch_shapes=[pltpu.VMEM((MLA_HEADS, tq, LANE), F32),
                        pltpu.VMEM((MLA_HEADS, tq, LANE), F32),
                        pltpu.VMEM((MLA_HEADS, tq, MLA_V), F32)],
    )
    return pl.pallas_call(
        _mla_attn_kernel,
        grid_spec=grid_spec,
        out_shape=jax.ShapeDtypeStruct((b, s, MLA_OUT), BF16),
        compiler_params=_params("parallel", "arbitrary"),
        name="mla_attn",
    )(qi_tab, kj_tab, q, kn, kr, v, gn[None, :])


def kernel(x, p, positions, pre_mix_norm, w_in, swa_sinks, rg_conv_w, rg_conv_b, rg_gate_a_w,
           rg_gate_a_b, rg_gate_x_w, rg_gate_x_b, rg_lambda, mla_q_norm, mla_w_uq, mla_kv_norm,
           mla_w_ukv, group_norm, w_out, post_mix_norm, pre_ffn_norm, w_gate, w_up, w_down,
           post_ffn_norm, w_ple, ple_norm, w_ple_gate, b_ple_gate):
    b, s, d = x.shape
    t = b * s
    depth = w_in.shape[0]

    pos_col = positions.astype(F32).reshape(t, 1)
    cos128, sin128 = _rope_tables(pos_col, HEAD_DIM)
    cos64, sin64 = _rope_tables(pos_col, MLA_ROPE)
    cos128_3, sin128_3 = cos128.reshape(b, s, LANE), sin128.reshape(b, s, LANE)

    xf = x.reshape(t, d)
    h = _norm_cast(xf, pre_mix_norm[0])
    for i in range(depth):
        z = _matmul([h], w_in, i, 1024, 512, n_cols=Z_MAIN, name="in_proj")
        w_kr = jnp.pad(w_in[i, :, Z_MAIN:], ((0, 0), (0, LANE - MLA_ROPE)))[None]
        kr = _matmul([h], w_kr, 0, 1024, LANE, name="in_proj_kr")
        z3 = z.reshape(b, s, Z_MAIN)
        gn = group_norm[i]

        o_a = _swa(z3, cos128_3, sin128_3, swa_sinks[i], gn[:SWA_OUT])
        o_b = _rglru(z3, rg_conv_w[i], rg_conv_b[i], rg_gate_a_w[i].astype(BF16), rg_gate_a_b[i],
                     rg_gate_x_w[i].astype(BF16), rg_gate_x_b[i], rg_lambda[i],
                     gn[SWA_OUT:SWA_OUT + RG_WIDTH])
        w_uq = mla_w_uq[i].astype(BF16).reshape(MLA_Q_RANK, MLA_HEADS, MLA_NOPE + MLA_ROPE)
        w_ukv = mla_w_ukv[i].astype(BF16).reshape(MLA_KV_RANK, MLA_HEADS, MLA_NOPE + MLA_V)
        q_c, kn_c, kr_c, v_c = _mla_proj(
            z, kr, cos64, sin64, mla_q_norm[i], mla_kv_norm[i],
            w_uq[:, :, :MLA_NOPE].reshape(MLA_Q_RANK, MLA_HEADS * MLA_NOPE),
            w_uq[:, :, MLA_NOPE:].reshape(MLA_Q_RANK, MLA_HEADS * MLA_ROPE),
            w_ukv[:, :, :MLA_NOPE].reshape(MLA_KV_RANK, MLA_HEADS * MLA_NOPE),
            w_ukv[:, :, MLA_NOPE:].reshape(MLA_KV_RANK, MLA_HEADS * MLA_V), b, s)
        o_c = _mla_attn(q_c, kn_c, kr_c, v_c, gn[SWA_OUT + RG_WIDTH:])

        mix = _matmul([o_a.reshape(t, SWA_OUT), o_b.reshape(t, RG_WIDTH), o_c.reshape(t, MLA_OUT)],
                      w_out, i, 1024, 512, name="out_proj")
        xf, h = _resid_norm(xf, mix[None], post_mix_norm[i], pre_ffn_norm[i])

        gu = _ffn_up(h, w_gate, w_up, i, 1024, 256)
        f = _matmul([gu], w_down, i, 512, 512, k_chunks=2, name="ffn_down")
        xf, xb = _resid_norm(xf, f, post_ffn_norm[i], None)

        gate = _matmul([xb], w_ple_gate, i, 1024, 512, name="ple_gate")
        g_next = pre_mix_norm[i + 1] if i + 1 < depth else None
        xf, h = _ple(xf, gate, p[i].reshape(t, PLE_DIM), w_ple[i].astype(BF16), ple_norm[i],
                     b_ple_gate[i], g_next)
    return xf.reshape(b, s, d)
```

```python
import functools
import math

import jax
import jax.numpy as jnp
import numpy as np
from jax import lax
from jax.experimental import pallas as pl
from jax.experimental.pallas import tpu as pltpu

F32 = jnp.float32
BF16 = jnp.bfloat16

D_MODEL = 4096
HEAD_DIM = 128
ROPE_THETA = 10000.0
NORM_EPS = 1e-6
PLE_DIM = 256
NEG_INF = -1e30

SWA_HEADS = 12
SWA_KV_HEADS = 4
SWA_GROUP = SWA_HEADS // SWA_KV_HEADS
SWA_BLOCK = 128
SWA_OUT = SWA_HEADS * HEAD_DIM
SWA_KV_OUT = SWA_KV_HEADS * HEAD_DIM

RG_WIDTH = 1024
RG_BLOCKS = 8
RG_BLOCK_DIM = RG_WIDTH // RG_BLOCKS
RG_CONV = 4
RG_C = 8.0

MLA_HEADS = 12
MLA_Q_RANK = 1024
MLA_KV_RANK = 512
MLA_NOPE = 128
MLA_ROPE = 64
MLA_V = 128
MLA_OUT = MLA_HEADS * MLA_V
MLA_QK_PAD = 256
MLA_Q_SCALE = (MLA_NOPE + MLA_ROPE) ** -0.5 * math.log2(math.e)

MIX_WIDTH = SWA_OUT + RG_WIDTH + MLA_OUT
Z_MAIN = SWA_OUT + 2 * SWA_KV_OUT + 2 * RG_WIDTH + MLA_Q_RANK + MLA_KV_RANK

V7X_VMEM_LIMIT_BYTES = 56 * 1024 * 1024
LANE = 128
SUBLANE = 8


def _params(*semantics):
    return pltpu.CompilerParams(dimension_semantics=semantics,
                                vmem_limit_bytes=V7X_VMEM_LIMIT_BYTES)


def _rms(xf, g):
    ms = jnp.mean(xf * xf, axis=-1, keepdims=True)
    return xf * lax.rsqrt(ms + NORM_EPS) * g


def _rope_table_kernel(pos_ref, inv_ref, sign_ref, cos_ref, sin_ref):
    ang = pos_ref[...] * inv_ref[...]
    cos_ref[...] = jnp.cos(ang)
    sin_ref[...] = jnp.sin(ang) * sign_ref[...]


def _rope_tables(pos_col, rot_dim):
    t = pos_col.shape[0]
    half = rot_dim // 2
    inv = ROPE_THETA ** (-jnp.arange(0, rot_dim, 2, dtype=F32) / rot_dim)
    inv_row = jnp.tile(inv, LANE // half)[None, :]
    sign_row = jnp.tile(jnp.concatenate([-jnp.ones((half,), F32), jnp.ones((half,), F32)]),
                        LANE // rot_dim)[None, :]
    tm = 1024
    row = pl.BlockSpec((1, LANE), lambda i: (0, 0))
    return pl.pallas_call(
        _rope_table_kernel,
        grid=(t // tm,),
        in_specs=[pl.BlockSpec((tm, 1), lambda i: (i, 0)), row, row],
        out_specs=[pl.BlockSpec((tm, LANE), lambda i: (i, 0))] * 2,
        out_shape=[jax.ShapeDtypeStruct((t, LANE), F32)] * 2,
        compiler_params=_params("parallel"),
        name="rope_tables",
    )(pos_col, inv_row, sign_row)


def _norm_cast_kernel(x_ref, g_ref, o_ref):
    o_ref[...] = _rms(x_ref[...], g_ref[...]).astype(o_ref.dtype)


def _norm_cast(x, g):
    t, d = x.shape
    tm = 256
    return pl.pallas_call(
        _norm_cast_kernel,
        grid=(t // tm,),
        in_specs=[pl.BlockSpec((tm, d), lambda i: (i, 0)),
                  pl.BlockSpec((1, d), lambda i: (0, 0))],
        out_specs=pl.BlockSpec((tm, d), lambda i: (i, 0)),
        out_shape=jax.ShapeDtypeStruct((t, d), BF16),
        compiler_params=_params("parallel"),
        name="norm_cast",
    )(x, g[None, :])


def _mm_kernel(*refs, k_sizes, w_transposed):
    n_in = len(k_sizes)
    w_ref, o_ref, wb_ref = refs[n_in], refs[n_in + 1], refs[n_in + 2]

    @pl.when(pl.program_id(2) == 0)
    def _():
        wb_ref[...] = w_ref[...].astype(BF16)

    acc = None
    off = 0
    for x_ref, ks in zip(refs[:n_in], k_sizes):
        if w_transposed:
            part = lax.dot_general(x_ref[...], wb_ref[:, off:off + ks], (((1,), (1,)), ((), ())),
                                   preferred_element_type=F32)
        else:
            part = jnp.dot(x_ref[...], wb_ref[off:off + ks, :], preferred_element_type=F32)
        acc = part if acc is None else acc + part
        off += ks
    o_ref[...] = acc.astype(o_ref.dtype)


def _matmul(xs, w_stack, layer, tm, tn, n_cols=None, k_chunks=1, out_dtype=F32,
            w_transposed=False, name="matmul"):
    m = xs[0].shape[0]
    if w_transposed:
        _, n, k = w_stack.shape
    else:
        _, k, n = w_stack.shape
    n_cols = n if n_cols is None else n_cols
    k_sizes = tuple(x.shape[1] for x in xs)
    assert m % tm == 0 and n_cols % tn == 0 and k % k_chunks == 0
    tk = k // k_chunks
    if k_chunks == 1:
        assert sum(k_sizes) == k
        in_specs = [pl.BlockSpec((tm, ks), lambda j, c, i: (i, 0)) for ks in k_sizes]
        out_specs = pl.BlockSpec((tm, tn), lambda j, c, i: (i, j))
        out_shape = jax.ShapeDtypeStruct((m, n_cols), out_dtype)
    else:
        assert k_sizes == (k,)
        k_sizes = (tk,)
        in_specs = [pl.BlockSpec((tm, tk), lambda j, c, i: (i, c))]
        out_specs = pl.BlockSpec((None, tm, tn), lambda j, c, i: (c, i, j))
        out_shape = jax.ShapeDtypeStruct((k_chunks, m, n_cols), out_dtype)
    if w_transposed:
        w_block = (tn, tk)
        in_specs.append(pl.BlockSpec((None, tn, tk), lambda j, c, i: (layer, j, c)))
    else:
        w_block = (tk, tn)
        in_specs.append(pl.BlockSpec((None, tk, tn), lambda j, c, i: (layer, c, j)))
    return pl.pallas_call(
        functools.partial(_mm_kernel, k_sizes=k_sizes, w_transposed=w_transposed),
        grid=(n_cols // tn, k_chunks, m // tm),
        in_specs=in_specs,
        out_specs=out_specs,
        out_shape=out_shape,
        scratch_shapes=[pltpu.VMEM(w_block, BF16)],
        compiler_params=_params("arbitrary", "arbitrary", "arbitrary"),
        name=name,
    )(*xs, w_stack)


def _ffn_up_kernel(x_ref, wg_ref, wu_ref, o_ref, wgb_ref, wub_ref):
    @pl.when(pl.program_id(1) == 0)
    def _():
        wgb_ref[...] = wg_ref[...].astype(BF16)
        wub_ref[...] = wu_ref[...].astype(BF16)

    x = x_ref[...]
    g = jnp.dot(x, wgb_ref[...], preferred_element_type=F32)
    u = jnp.dot(x, wub_ref[...], preferred_element_type=F32)
    o_ref[...] = (g * jax.nn.sigmoid(g) * u).astype(o_ref.dtype)


def _ffn_up(h, wg_stack, wu_stack, layer, tm, tn):
    m, k = h.shape
    n = wg_stack.shape[-1]
    assert m % tm == 0 and n % tn == 0
    wspec = pl.BlockSpec((None, k, tn), lambda j, i: (layer, 0, j))
    return pl.pallas_call(
        _ffn_up_kernel,
        grid=(n // tn, m // tm),
        in_specs=[pl.BlockSpec((tm, k), lambda j, i: (i, 0)), wspec, wspec],
        out_specs=pl.BlockSpec((tm, tn), lambda j, i: (i, j)),
        out_shape=jax.ShapeDtypeStruct((m, n), BF16),
        scratch_shapes=[pltpu.VMEM((k, tn), BF16), pltpu.VMEM((k, tn), BF16)],
        compiler_params=_params("arbitrary", "arbitrary"),
        name="ffn_up",
    )(h, wg_stack, wu_stack)


def _wstream_kernel(*refs, k_sizes, n_w, layer, w_transposed, tn, n_total, rc, n_chunks, n_blocks,
                    swiglu):
    n_x = len(k_sizes)
    x_refs = refs[:n_x]
    w_hbm = refs[n_x:n_x + n_w]
    o_ref = refs[n_x + n_w]
    wb = refs[n_x + n_w + 1:n_x + 2 * n_w + 1]
    st = refs[n_x + 2 * n_w + 1:n_x + 3 * n_w + 1]
    sem = refs[n_x + 3 * n_w + 1]
    j, i = pl.program_id(0), pl.program_id(1)
    q = j * n_chunks + i
    slot = j % 2
    shift = (tn - n_total % tn) % tn

    def col0(blk):
        return pl.multiple_of(jnp.minimum(blk * tn, n_total - tn), LANE)

    def chunk_copy(w, blk, ch, buf):
        r0 = pl.multiple_of(ch * rc, SUBLANE)
        if w_transposed:
            src = w_hbm[w].at[layer, pl.ds(pl.multiple_of(col0(blk) + r0, SUBLANE), rc), :]
        else:
            src = w_hbm[w].at[layer, pl.ds(r0, rc), pl.ds(col0(blk), tn)]
        return pltpu.make_async_copy(src, st[w].at[buf], sem.at[w, buf])

    def cast_chunk(w, ch, buf, dst_slot):
        r0 = pl.multiple_of(ch * rc, 2 * SUBLANE)
        wb[w][dst_slot, pl.ds(r0, rc), :] = st[w][buf].astype(BF16)

    @pl.when(q == 0)
    def _():
        for w in range(n_w):
            chunk_copy(w, 0, 0, 0).start()
        for ch in range(n_chunks):
            buf = ch % 2
            for w in range(n_w):
                if ch + 1 < n_chunks:
                    chunk_copy(w, 0, ch + 1, 1 - buf).start()
                chunk_copy(w, 0, ch, buf).wait()
                cast_chunk(w, ch, buf, 0)
        for w in range(n_w):
            chunk_copy(w, 0, n_chunks - 1, 1).start()

    nxt = (j + 1) % n_blocks
    pbuf = (q + 1) % 2
    pch = (i + n_chunks - 1) % n_chunks
    pblk = jnp.where(i >= 1, nxt, j)
    pslot = jnp.where(i >= 1, 1 - slot, slot)
    for w in range(n_w):
        chunk_copy(w, pblk, pch, pbuf).wait()
    for w in range(n_w):
        chunk_copy(w, nxt, i, q % 2).start()
    for w in range(n_w):
        cast_chunk(w, pch, pbuf, pslot)

    res = []
    for w in range(n_w):
        acc = None
        off = 0
        for x_ref, ks in zip(x_refs, k_sizes):
            if w_transposed:
                part = lax.dot_general(x_ref[...], wb[w][slot, :, off:off + ks],
                                       (((1,), (1,)), ((), ())), preferred_element_type=F32)
            else:
                part = jnp.dot(x_ref[...], wb[w][slot, off:off + ks, :], preferred_element_type=F32)
            acc = part if acc is None else acc + part
            off += ks
        res.append(acc)
    out = (res[0] * jax.nn.sigmoid(res[0]) * res[1]) if swiglu else res[0]
    out = out.astype(o_ref.dtype)
    if shift == 0:
        o_ref[...] = out
    else:
        @pl.when(j != n_blocks - 1)
        def _():
            o_ref[...] = out

        @pl.when(j == n_blocks - 1)
        def _():
            o_ref[...] = jnp.concatenate([out[:, shift:], out[:, :shift]], axis=1)

    @pl.when(q == n_blocks * n_chunks - 1)
    def _():
        for w in range(n_w):
            chunk_copy(w, nxt, i, q % 2).wait()


def _matmul_wstream(xs, w_stacks, layer, tm, tn, n_cols=None, k_sizes=None, out_dtype=F32,
                    w_transposed=False, swiglu=False, name="matmul"):
    m = xs[0].shape[0]
    k_sizes = tuple(x.shape[1] for x in xs) if k_sizes is None else tuple(k_sizes)
    k = sum(k_sizes)
    if w_transposed:
        _, n, kw = w_stacks[0].shape
        rows, cols = tn, k
    else:
        _, kw, n = w_stacks[0].shape
        rows, cols = k, tn
    n_cols = n if n_cols is None else n_cols
    n_chunks = m // tm
    rc = rows // n_chunks
    n_blocks = -(-n_cols // tn)
    n_w = len(w_stacks)
    assert kw == k and m % tm == 0 and rows % n_chunks == 0 and rc % (2 * SUBLANE) == 0
    assert n_blocks >= 2 and n_cols >= tn and (not w_transposed or n_cols % tn == 0)
    assert n_w == (2 if swiglu else 1)
    in_specs = [pl.BlockSpec((tm, ks), lambda j, i: (i, 0)) for ks in k_sizes]
    in_specs += [pl.BlockSpec(memory_space=pl.ANY)] * n_w
    scratch = [pltpu.VMEM((2, rows, cols), BF16)] * n_w
    scratch += [pltpu.VMEM((2, rc, cols), F32)] * n_w
    scratch.append(pltpu.SemaphoreType.DMA((n_w, 2)))
    return pl.pallas_call(
        functools.partial(_wstream_kernel, k_sizes=k_sizes, n_w=n_w, layer=layer,
                          w_transposed=w_transposed, tn=tn, n_total=n_cols, rc=rc,
                          n_chunks=n_chunks, n_blocks=n_blocks, swiglu=swiglu),
        grid=(n_blocks, n_chunks),
        in_specs=in_specs,
        out_specs=pl.BlockSpec((tm, tn), lambda j, i: (i, j)),
        out_shape=jax.ShapeDtypeStruct((m, n_blocks * tn), out_dtype),
        scratch_shapes=scratch,
        compiler_params=_params("arbitrary", "arbitrary"),
        name=name,
    )(*xs, *w_stacks)


def _resid_norm_kernel(x_ref, y_ref, gp_ref, gn_ref, xo_ref, ho_ref, *, norm_next):
    y = y_ref[0]
    for c in range(1, y_ref.shape[0]):
        y = y + y_ref[c]
    xn = x_ref[...] + _rms(y, gp_ref[...])
    xo_ref[...] = xn
    if norm_next:
        ho_ref[...] = _rms(xn, gn_ref[...]).astype(ho_ref.dtype)
    else:
        ho_ref[...] = xn.astype(ho_ref.dtype)


def _resid_norm(x, y_parts, g_post, g_next):
    t, d = x.shape
    parts = y_parts.shape[0]
    tm = 128
    norm_next = g_next is not None
    if g_next is None:
        g_next = g_post
    big = pl.BlockSpec((tm, d), lambda i: (i, 0))
    row = pl.BlockSpec((1, d), lambda i: (0, 0))
    return pl.pallas_call(
        functools.partial(_resid_norm_kernel, norm_next=norm_next),
        grid=(t // tm,),
        in_specs=[big, pl.BlockSpec((parts, tm, d), lambda i: (0, i, 0)), row, row],
        out_specs=[big, big],
        out_shape=[jax.ShapeDtypeStruct((t, d), F32), jax.ShapeDtypeStruct((t, d), BF16)],
        compiler_params=_params("parallel"),
        name="resid_norm",
    )(x, y_parts, g_post[None, :], g_next[None, :])


def _ple_kernel(x_ref, gate_ref, p_ref, wp_ref, pn_ref, b_ref, gn_ref, xo_ref, *maybe_ho_ref):
    e = _rms(jnp.dot(p_ref[...].astype(BF16), wp_ref[...], preferred_element_type=F32), pn_ref[...])
    xn = x_ref[...] + jax.nn.sigmoid(gate_ref[...] + b_ref[...]) * e
    xo_ref[...] = xn
    for ho_ref in maybe_ho_ref:
        ho_ref[...] = _rms(xn, gn_ref[...]).astype(ho_ref.dtype)


def _ple(x, gate, p, w_ple, ple_norm, b_gate, g_next):
    t, d = x.shape
    tm = 256
    norm_next = g_next is not None
    if g_next is None:
        g_next = ple_norm
    big = pl.BlockSpec((tm, d), lambda i: (i, 0))
    row = pl.BlockSpec((1, d), lambda i: (0, 0))
    out_specs = [big, big] if norm_next else [big]
    out_shape = [jax.ShapeDtypeStruct((t, d), F32)]
    if norm_next:
        out_shape.append(jax.ShapeDtypeStruct((t, d), BF16))
    outs = pl.pallas_call(
        _ple_kernel,
        grid=(t // tm,),
        in_specs=[big, big, pl.BlockSpec((tm, PLE_DIM), lambda i: (i, 0)),
                  pl.BlockSpec((PLE_DIM, d), lambda i: (0, 0)), row, row, row],
        out_specs=out_specs,
        out_shape=out_shape,
        compiler_params=_params("parallel"),
        name="ple",
    )(x, gate, p, w_ple, ple_norm[None, :], b_gate[None, :], g_next[None, :])
    return (outs[0], outs[1]) if norm_next else (outs[0], None)


def _swa_kernel(sink_ref, q_ref, kc_ref, vc_ref, vp_ref, cos_ref, sin_ref, gn_ref, o_ref,
                kprev_ref):
    blk = SWA_BLOCK
    n = pl.program_id(1)
    cos, sin = cos_ref[...], sin_ref[...]

    @pl.when(n == 0)
    def _():
        kprev_ref[...] = jnp.zeros_like(kprev_ref)

    def rope(x):
        return x * cos + pltpu.roll(x, HEAD_DIM // 2, axis=1) * sin

    row = lax.broadcasted_iota(jnp.int32, (blk, 2 * blk), 0)
    col = lax.broadcasted_iota(jnp.int32, (blk, 2 * blk), 1)
    dist = blk + row - col
    kpos = (n - 1) * blk + col
    valid = (dist >= 0) & (dist < blk) & (kpos >= 0)
    scale = HEAD_DIM ** -0.5

    outs = []
    sq = jnp.zeros((blk, HEAD_DIM), F32)
    for kv in range(SWA_KV_HEADS):
        ks = slice(kv * HEAD_DIM, (kv + 1) * HEAD_DIM)
        k_cur = rope(kc_ref[:, ks]).astype(BF16)
        k = jnp.concatenate([kprev_ref[kv], k_cur], axis=0)
        kprev_ref[kv] = k_cur
        v = jnp.concatenate([vp_ref[:, ks], vc_ref[:, ks]], axis=0).astype(BF16)
        for g in range(SWA_GROUP):
            h = kv * SWA_GROUP + g
            q = rope(q_ref[:, h * HEAD_DIM:(h + 1) * HEAD_DIM]).astype(BF16)
            s = lax.dot_general(q, k, (((1,), (1,)), ((), ())), preferred_element_type=F32) * scale
            s = jnp.where(valid, s, NEG_INF)
            sink = sink_ref[h]
            m = jnp.maximum(jnp.max(s, axis=-1, keepdims=True), sink)
            p = jnp.exp(s - m)
            denom = jnp.sum(p[:, :blk] + p[:, blk:], axis=-1, keepdims=True) + jnp.exp(sink - m)
            o = jnp.dot(p.astype(BF16), v, preferred_element_type=F32) / denom
            outs.append(o)
            sq = sq + o * o
    inv = lax.rsqrt(jnp.sum(sq, axis=-1, keepdims=True) / SWA_OUT + NORM_EPS)
    for h, o in enumerate(outs):
        hs = slice(h * HEAD_DIM, (h + 1) * HEAD_DIM)
        o_ref[:, hs] = (o * inv * gn_ref[:, hs]).astype(o_ref.dtype)


def _swa(z3, cos, sin, sinks, gn):
    b, s, _ = z3.shape
    blk = SWA_BLOCK
    nb = s // blk
    kcol = SWA_OUT // SWA_KV_OUT
    vcol = kcol + 1
    cur = lambda bi, n: (bi, n, 0)
    tab = pl.BlockSpec((None, blk, LANE), cur)
    return pl.pallas_call(
        _swa_kernel,
        grid=(b, nb),
        in_specs=[
            pl.BlockSpec(memory_space=pltpu.SMEM),
            pl.BlockSpec((None, blk, SWA_OUT), cur),
            pl.BlockSpec((None, blk, SWA_KV_OUT), lambda bi, n: (bi, n, kcol)),
            pl.BlockSpec((None, blk, SWA_KV_OUT), lambda bi, n: (bi, n, vcol)),
            pl.BlockSpec((None, blk, SWA_KV_OUT), lambda bi, n: (bi, jnp.maximum(n - 1, 0), vcol)),
            tab, tab,
            pl.BlockSpec((1, SWA_OUT), lambda bi, n: (0, 0)),
        ],
        out_specs=pl.BlockSpec((None, blk, SWA_OUT), cur),
        out_shape=jax.ShapeDtypeStruct((b, s, SWA_OUT), BF16),
        scratch_shapes=[pltpu.VMEM((SWA_KV_HEADS, blk, HEAD_DIM), BF16)],
        compiler_params=_params("arbitrary", "arbitrary"),
        name="swa",
    )(sinks, z3, z3, z3, z3, cos, sin, gn[None, :])


RG_CHUNK = 256


def _rglru_kernel(x0_ref, x1_ref, g0_ref, g1_ref, cw_ref, cb_ref, wa_ref, ba_ref, wx_ref, bx_ref,
                  lam_ref, gn_ref, o_ref, xe_ref, a_ref, b_ref, h_ref, carry_ref):
    lc = RG_CHUNK
    pad = SUBLANE

    @pl.when(pl.program_id(1) == 0)
    def _():
        xe_ref[0:pad, :] = jnp.zeros((pad, RG_WIDTH), F32)
        carry_ref[...] = jnp.zeros_like(carry_ref)

    xe_ref[pad:pad + lc, :] = jnp.concatenate([x0_ref[...], x1_ref[...]], axis=1)
    cw = cw_ref[...]
    xc = cb_ref[...]
    for j in range(RG_CONV):
        sh = RG_CONV - 1 - j
        xc = xc + xe_ref[pad - sh:pad - sh + lc, :] * cw[j:j + 1, :]
    xe_ref[0:pad, :] = xe_ref[lc:lc + pad, :]

    xcb = xc.astype(BF16)

    def gate(w_ref, bias_ref):
        parts = [jnp.dot(xcb[:, nb * RG_BLOCK_DIM:(nb + 1) * RG_BLOCK_DIM], w_ref[nb],
                         preferred_element_type=F32) for nb in range(RG_BLOCKS)]
        return jax.nn.sigmoid(jnp.concatenate(parts, axis=1) + bias_ref[...])

    r = gate(wa_ref, ba_ref)
    i = gate(wx_ref, bx_ref)
    nlam = -lam_ref[...]
    softplus = jnp.maximum(nlam, 0.0) + jnp.log1p(jnp.exp(-jnp.abs(nlam)))
    log_a = (-RG_C * softplus) * r
    a = jnp.exp(log_a)
    one_minus_a2 = -jnp.tanh(log_a) * (a * a + 1.0)
    a_ref[...] = a
    b_ref[...] = jnp.sqrt(one_minus_a2) * i * xc

    sub = lax.broadcasted_iota(jnp.int32, (SUBLANE, RG_WIDTH), 0)

    def scan_tile(t, h):
        r0 = pl.multiple_of(t * SUBLANE, SUBLANE)
        at = a_ref[pl.ds(r0, SUBLANE), :]
        bt = b_ref[pl.ds(r0, SUBLANE), :]
        for sh in (1, 2, 4):
            keep = sub >= sh
            a_sh = pltpu.roll(at, sh, axis=0)
            b_sh = pltpu.roll(bt, sh, axis=0)
            bt = jnp.where(keep, at * b_sh + bt, bt)
            at = jnp.where(keep, at * a_sh, at)
        ht = at * h + bt
        h_ref[pl.ds(r0, SUBLANE), :] = ht
        return ht[SUBLANE - 1:SUBLANE, :]

    carry_ref[0:1, :] = lax.fori_loop(0, lc // SUBLANE, scan_tile, carry_ref[0:1, :])

    g = jnp.concatenate([g0_ref[...], g1_ref[...]], axis=1)
    y = h_ref[...] * jax.nn.gelu(g)
    o_ref[...] = _rms(y, gn_ref[...]).astype(o_ref.dtype)


def _rglru(z3, conv_w, conv_b, wa, ba, wx, bx, lam, gn):
    b, s, _ = z3.shape
    lc = RG_CHUNK
    half = RG_WIDTH // 2
    x_col = (SWA_OUT + 2 * SWA_KV_OUT) // half
    g_col = x_col + 2

    def zcol(c):
        return pl.BlockSpec((None, lc, half), lambda bi, ci: (bi, ci, c))

    row = pl.BlockSpec((1, RG_WIDTH), lambda bi, ci: (0, 0))
    wspec = pl.BlockSpec((RG_BLOCKS, RG_BLOCK_DIM, RG_BLOCK_DIM), lambda bi, ci: (0, 0, 0))
    return pl.pallas_call(
        _rglru_kernel,
        grid=(b, s // lc),
        in_specs=[zcol(x_col), zcol(x_col + 1), zcol(g_col), zcol(g_col + 1),
                  pl.BlockSpec((RG_CONV, RG_WIDTH), lambda bi, ci: (0, 0)), row,
                  wspec, row, wspec, row, row, row],
        out_specs=pl.BlockSpec((None, lc, RG_WIDTH), lambda bi, ci: (bi, ci, 0)),
        out_shape=jax.ShapeDtypeStruct((b, s, RG_WIDTH), BF16),
        scratch_shapes=[pltpu.VMEM((lc + SUBLANE, RG_WIDTH), F32),
                        pltpu.VMEM((lc, RG_WIDTH), F32),
                        pltpu.VMEM((lc, RG_WIDTH), F32),
                        pltpu.VMEM((lc, RG_WIDTH), F32),
                        pltpu.VMEM((SUBLANE, RG_WIDTH), F32)],
        compiler_params=_params("parallel", "arbitrary"),
        name="rglru",
    )(z3, z3, z3, z3, conv_w, conv_b[None, :], wa, ba[None, :], wx, bx[None, :],
      lam[None, :], gn[None, :])


def _rope64(x, cos, sin, first_half):
    swapped = jnp.where(first_half, pltpu.roll(x, LANE - MLA_ROPE // 2, axis=1),
                        pltpu.roll(x, MLA_ROPE // 2, axis=1))
    return x * cos + swapped * sin


def _mla_proj_kernel(cq0_ref, cq1_ref, ckv_ref, kr_ref, cos_ref, sin_ref, qn_ref, kvn_ref,
                     wqn_ref, wqr_ref, wkn_ref, wv_ref, q_out, kn_out, kr_out, v_out):
    cos, sin = cos_ref[...], sin_ref[...]
    tm = cos.shape[0]
    lane = lax.broadcasted_iota(jnp.int32, (tm, LANE), 1)
    first_half = (lane % MLA_ROPE) < (MLA_ROPE // 2)
    low = lane < MLA_ROPE

    hq = _rms(jnp.concatenate([cq0_ref[...], cq1_ref[...]], axis=1), qn_ref[...]).astype(BF16)
    qn = jnp.dot(hq, wqn_ref[...], preferred_element_type=F32) * MLA_Q_SCALE
    qr = jnp.dot(hq, wqr_ref[...], preferred_element_type=F32) * MLA_Q_SCALE
    for pair in range(MLA_HEADS // 2):
        rot = _rope64(qr[:, pair * LANE:(pair + 1) * LANE], cos, sin, first_half)
        for odd in range(2):
            h = 2 * pair + odd
            piece = pltpu.roll(rot, MLA_ROPE, axis=1) if odd else rot
            q_out[h, :, 0:MLA_NOPE] = qn[:, h * MLA_NOPE:(h + 1) * MLA_NOPE].astype(BF16)
            q_out[h, :, MLA_NOPE:MLA_QK_PAD] = jnp.where(low, piece, 0.0).astype(BF16)

    hkv = _rms(ckv_ref[...], kvn_ref[...]).astype(BF16)
    kn = jnp.dot(hkv, wkn_ref[...], preferred_element_type=F32)
    v = jnp.dot(hkv, wv_ref[...], preferred_element_type=F32)
    for h in range(MLA_HEADS):
        hs = slice(h * LANE, (h + 1) * LANE)
        kn_out[h] = kn[:, hs].astype(BF16)
        v_out[h] = v[:, hs].astype(BF16)
    kr_out[...] = _rope64(kr_ref[...], cos, sin, first_half).astype(BF16)


def _mla_proj(z, kr, cos64, sin64, q_norm, kv_norm, wqn, wqr, wkn, wv, b, s):
    t = z.shape[0]
    tm = 256
    per_b = s // tm
    half = MLA_Q_RANK // 2
    cq_col = (SWA_OUT + 2 * SWA_KV_OUT + 2 * RG_WIDTH) // half
    tok = lambda i: (i, 0)
    head_blk = lambda i: (i // per_b, 0, i % per_b, 0)
    full = lambda shape: pl.BlockSpec(shape, lambda i: (0, 0))
    return pl.pallas_call(
        _mla_proj_kernel,
        grid=(t // tm,),
        in_specs=[pl.BlockSpec((tm, half), lambda i: (i, cq_col)),
                  pl.BlockSpec((tm, half), lambda i: (i, cq_col + 1)),
                  pl.BlockSpec((tm, MLA_KV_RANK), lambda i: (i, cq_col + 2)),
                  pl.BlockSpec((tm, LANE), tok), pl.BlockSpec((tm, LANE), tok),
                  pl.BlockSpec((tm, LANE), tok),
                  full((1, MLA_Q_RANK)), full((1, MLA_KV_RANK)),
                  full(wqn.shape), full(wqr.shape), full(wkn.shape), full(wv.shape)],
        out_specs=[pl.BlockSpec((None, MLA_HEADS, tm, MLA_QK_PAD), head_blk),
                   pl.BlockSpec((None, MLA_HEADS, tm, LANE), head_blk),
                   pl.BlockSpec((None, tm, LANE), lambda i: (i // per_b, i % per_b, 0)),
                   pl.BlockSpec((None, MLA_HEADS, tm, LANE), head_blk)],
        out_shape=[jax.ShapeDtypeStruct((b, MLA_HEADS, s, MLA_QK_PAD), BF16),
                   jax.ShapeDtypeStruct((b, MLA_HEADS, s, LANE), BF16),
                   jax.ShapeDtypeStruct((b, s, LANE), BF16),
                   jax.ShapeDtypeStruct((b, MLA_HEADS, s, LANE), BF16)],
        compiler_params=_params("parallel"),
        name="mla_proj",
    )(z, z, z, kr, cos64, sin64, q_norm[None, :], kv_norm[None, :], wqn, wqr, wkn, wv)


MLA_TQ = 256


def _mla_attn_kernel(qi_tab, kj_tab, q_ref, kn_ref, kr_ref, v_ref, gn_ref, o_ref,
                     m_ref, l_ref, acc_ref):
    tq = MLA_TQ
    step = pl.program_id(1)
    qi = qi_tab[step]
    kj = kj_tab[step]

    @pl.when(kj == 0)
    def _():
        m_ref[...] = jnp.full(m_ref.shape, NEG_INF, F32)
        l_ref[...] = jnp.zeros_like(l_ref)
        acc_ref[...] = jnp.zeros_like(acc_ref)

    row = lax.broadcasted_iota(jnp.int32, (tq, tq), 0)
    col = lax.broadcasted_iota(jnp.int32, (tq, tq), 1)
    causal = col + (kj - qi) * tq <= row
    kr = kr_ref[...]
    for h in range(MLA_HEADS):
        k = jnp.concatenate([kn_ref[h], kr], axis=1)
        s = lax.dot_general(q_ref[h], k, (((1,), (1,)), ((), ())), preferred_element_type=F32)
        s = jnp.where(causal, s, NEG_INF)
        m_old = m_ref[h]
        m_new = jnp.maximum(m_old, jnp.max(s, axis=-1, keepdims=True))
        alpha = jnp.exp2(m_old - m_new)
        p = jnp.exp2(s - jnp.concatenate([m_new, m_new], axis=1))
        l_ref[h] = alpha * l_ref[h] + (p[:, :LANE] + p[:, LANE:])
        acc_ref[h] = alpha * acc_ref[h] + jnp.dot(p.astype(BF16), v_ref[h],
                                                  preferred_element_type=F32)
        m_ref[h] = m_new

    @pl.when(kj == qi)
    def _():
        sq = jnp.zeros((tq, MLA_V), F32)
        for h in range(MLA_HEADS):
            o = acc_ref[h] / jnp.sum(l_ref[h], axis=-1, keepdims=True)
            acc_ref[h] = o
            sq = sq + o * o
        inv = lax.rsqrt(jnp.sum(sq, axis=-1, keepdims=True) / MLA_OUT + NORM_EPS)
        for h in range(MLA_HEADS):
            hs = slice(h * MLA_V, (h + 1) * MLA_V)
            o_ref[:, hs] = (acc_ref[h] * inv * gn_ref[:, hs]).astype(o_ref.dtype)


def _mla_attn(q, kn, kr, v, gn):
    b, _, s, _ = q.shape
    tq = MLA_TQ
    nq = s // tq
    pairs = [(qi, kj) for qi in range(nq) for kj in range(qi + 1)]
    qi_tab = jnp.asarray(np.array([pr[0] for pr in pairs], np.int32))
    kj_tab = jnp.asarray(np.array([pr[1] for pr in pairs], np.int32))
    head_rows = lambda tab_of: (lambda bi, st, qt, kt: (bi, 0, tab_of(qt, kt)[st], 0))
    by_q = lambda qt, kt: qt
    by_k = lambda qt, kt: kt
    grid_spec = pltpu.PrefetchScalarGridSpec(
        num_scalar_prefetch=2,
        grid=(b, len(pairs)),
        in_specs=[pl.BlockSpec((None, MLA_HEADS, tq, MLA_QK_PAD), head_rows(by_q)),
                  pl.BlockSpec((None, MLA_HEADS, tq, LANE), head_rows(by_k)),
                  pl.BlockSpec((None, tq, LANE), lambda bi, st, qt, kt: (bi, kt[st], 0)),
                  pl.BlockSpec((None, MLA_HEADS, tq, LANE), head_rows(by_k)),
                  pl.BlockSpec((1, MLA_OUT), lambda bi, st, qt, kt: (0, 0))],
        out_specs=pl.BlockSpec((None, tq, MLA_OUT), lambda bi, st, qt, kt: (bi, qt[st], 0)),
        scratch_shapes=[pltpu.VMEM((MLA_HEADS, tq, LANE), F32),
                        pltpu.VMEM((MLA_HEADS, tq, LANE), F32),
                        pltpu.VMEM((MLA_HEADS, tq, MLA_V), F32)],
    )
    return pl.pallas_call(
        _mla_attn_kernel,
        grid_spec=grid_spec,
        out_shape=jax.ShapeDtypeStruct((b, s, MLA_OUT), BF16),
        compiler_params=_params("parallel", "arbitrary"),
        name="mla_attn",
    )(qi_tab, kj_tab, q, kn, kr, v, gn[None, :])


def kernel(x, p, positions, pre_mix_norm, w_in, swa_sinks, rg_conv_w, rg_conv_b, rg_gate_a_w,
           rg_gate_a_b, rg_gate_x_w, rg_gate_x_b, rg_lambda, mla_q_norm, mla_w_uq, mla_kv_norm,
           mla_w_ukv, group_norm, w_out, post_mix_norm, pre_ffn_norm, w_gate, w_up, w_down,
           post_ffn_norm, w_ple, ple_norm, w_ple_gate, b_ple_gate):
    b, s, d = x.shape
    t = b * s
    depth = w_in.shape[0]

    pos_col = positions.astype(F32).reshape(t, 1)
    cos128, sin128 = _rope_tables(pos_col, HEAD_DIM)
    cos64, sin64 = _rope_tables(pos_col, MLA_ROPE)
    cos128_3, sin128_3 = cos128.reshape(b, s, LANE), sin128.reshape(b, s, LANE)

    w_in_t = jnp.swapaxes(w_in, 1, 2)
    xf = x.reshape(t, d)
    h = _norm_cast(xf, pre_mix_norm[0])
    for i in range(depth):
        z = _matmul_wstream([h], [w_in_t], i, 1024, 1024, n_cols=Z_MAIN, w_transposed=True,
                            name="in_proj")
        w_kr = jnp.pad(w_in_t[i, Z_MAIN:, :], ((0, LANE - MLA_ROPE), (0, 0)))[None]
        kr = _matmul([h], w_kr, 0, 1024, LANE, w_transposed=True, name="in_proj_kr")
        z3 = z.reshape(b, s, Z_MAIN)
        gn = group_norm[i]

        o_a = _swa(z3, cos128_3, sin128_3, swa_sinks[i], gn[:SWA_OUT])
        o_b = _rglru(z3, rg_conv_w[i], rg_conv_b[i], rg_gate_a_w[i].astype(BF16), rg_gate_a_b[i],
                     rg_gate_x_w[i].astype(BF16), rg_gate_x_b[i], rg_lambda[i],
                     gn[SWA_OUT:SWA_OUT + RG_WIDTH])
        w_uq = mla_w_uq[i].astype(BF16).reshape(MLA_Q_RANK, MLA_HEADS, MLA_NOPE + MLA_ROPE)
        w_ukv = mla_w_ukv[i].astype(BF16).reshape(MLA_KV_RANK, MLA_HEADS, MLA_NOPE + MLA_V)
        q_c, kn_c, kr_c, v_c = _mla_proj(
            z, kr, cos64, sin64, mla_q_norm[i], mla_kv_norm[i],
            w_uq[:, :, :MLA_NOPE].reshape(MLA_Q_RANK, MLA_HEADS * MLA_NOPE),
            w_uq[:, :, MLA_NOPE:].reshape(MLA_Q_RANK, MLA_HEADS * MLA_ROPE),
            w_ukv[:, :, :MLA_NOPE].reshape(MLA_KV_RANK, MLA_HEADS * MLA_NOPE),
            w_ukv[:, :, MLA_NOPE:].reshape(MLA_KV_RANK, MLA_HEADS * MLA_V), b, s)
        o_c = _mla_attn(q_c, kn_c, kr_c, v_c, gn[SWA_OUT + RG_WIDTH:])

        mix = _matmul_wstream(
            [o_a.reshape(t, SWA_OUT), o_b.reshape(t, RG_WIDTH), o_c.reshape(t, MLA_OUT)],
            [w_out], i, 1024, 1024, name="out_proj")
        xf, h = _resid_norm(xf, mix[None], post_mix_norm[i], pre_ffn_norm[i])

        gu = _matmul_wstream([h], [w_gate, w_up], i, 1024, 512, out_dtype=BF16, swiglu=True,
                             name="ffn_up")
        f = _matmul_wstream([gu], [w_down], i, 512, 512, k_sizes=[w_down.shape[1]], name="ffn_down")
        xf, xb = _resid_norm(xf, f[None], post_ffn_norm[i], None)

        gate = _matmul_wstream([xb], [w_ple_gate], i, 1024, 1024, name="ple_gate")
        g_next = pre_mix_norm[i + 1] if i + 1 < depth else None
        xf, h = _ple(xf, gate, p[i].reshape(t, PLE_DIM), w_ple[i].astype(BF16), ple_norm[i],
                     b_ple_gate[i], g_next)
    return xf.reshape(b, s, d)
```

```python
import functools
import math

import jax
import jax.numpy as jnp
import numpy as np
from jax import lax
from jax.experimental import pallas as pl
from jax.experimental.pallas import tpu as pltpu

F32 = jnp.float32
BF16 = jnp.bfloat16

D_MODEL = 4096
HEAD_DIM = 128
ROPE_THETA = 10000.0
NORM_EPS = 1e-6
PLE_DIM = 256
NEG_INF = -1e30

SWA_HEADS = 12
SWA_KV_HEADS = 4
SWA_GROUP = SWA_HEADS // SWA_KV_HEADS
SWA_BLOCK = 128
SWA_OUT = SWA_HEADS * HEAD_DIM
SWA_KV_OUT = SWA_KV_HEADS * HEAD_DIM

RG_WIDTH = 1024
RG_BLOCKS = 8
RG_BLOCK_DIM = RG_WIDTH // RG_BLOCKS
RG_CONV = 4
RG_C = 8.0

MLA_HEADS = 12
MLA_Q_RANK = 1024
MLA_KV_RANK = 512
MLA_NOPE = 128
MLA_ROPE = 64
MLA_V = 128
MLA_OUT = MLA_HEADS * MLA_V
MLA_QK_PAD = 256
MLA_Q_SCALE = (MLA_NOPE + MLA_ROPE) ** -0.5 * math.log2(math.e)

MIX_WIDTH = SWA_OUT + RG_WIDTH + MLA_OUT
Z_MAIN = SWA_OUT + 2 * SWA_KV_OUT + 2 * RG_WIDTH + MLA_Q_RANK + MLA_KV_RANK

V7X_VMEM_LIMIT_BYTES = 56 * 1024 * 1024
LANE = 128
SUBLANE = 8


def _params(*semantics):
    return pltpu.CompilerParams(dimension_semantics=semantics,
                                vmem_limit_bytes=V7X_VMEM_LIMIT_BYTES)


def _rms(xf, g):
    ms = jnp.mean(xf * xf, axis=-1, keepdims=True)
    return xf * lax.rsqrt(ms + NORM_EPS) * g


def _rope_table_kernel(pos_ref, inv_ref, sign_ref, cos_ref, sin_ref):
    ang = pos_ref[...] * inv_ref[...]
    cos_ref[...] = jnp.cos(ang)
    sin_ref[...] = jnp.sin(ang) * sign_ref[...]


def _rope_tables(pos_col, rot_dim):
    t = pos_col.shape[0]
    half = rot_dim // 2
    inv = ROPE_THETA ** (-jnp.arange(0, rot_dim, 2, dtype=F32) / rot_dim)
    inv_row = jnp.tile(inv, LANE // half)[None, :]
    sign_row = jnp.tile(jnp.concatenate([-jnp.ones((half,), F32), jnp.ones((half,), F32)]),
                        LANE // rot_dim)[None, :]
    tm = 1024
    row = pl.BlockSpec((1, LANE), lambda i: (0, 0))
    return pl.pallas_call(
        _rope_table_kernel,
        grid=(t // tm,),
        in_specs=[pl.BlockSpec((tm, 1), lambda i: (i, 0)), row, row],
        out_specs=[pl.BlockSpec((tm, LANE), lambda i: (i, 0))] * 2,
        out_shape=[jax.ShapeDtypeStruct((t, LANE), F32)] * 2,
        compiler_params=_params("parallel"),
        name="rope_tables",
    )(pos_col, inv_row, sign_row)


def _norm_cast_kernel(x_ref, g_ref, o_ref):
    o_ref[...] = _rms(x_ref[...], g_ref[...]).astype(o_ref.dtype)


def _norm_cast(x, g):
    t, d = x.shape
    tm = 256
    return pl.pallas_call(
        _norm_cast_kernel,
        grid=(t // tm,),
        in_specs=[pl.BlockSpec((tm, d), lambda i: (i, 0)),
                  pl.BlockSpec((1, d), lambda i: (0, 0))],
        out_specs=pl.BlockSpec((tm, d), lambda i: (i, 0)),
        out_shape=jax.ShapeDtypeStruct((t, d), BF16),
        compiler_params=_params("parallel"),
        name="norm_cast",
    )(x, g[None, :])


def _mm_kernel(*refs, k_sizes, w_transposed):
    n_in = len(k_sizes)
    w_ref, o_ref, wb_ref = refs[n_in], refs[n_in + 1], refs[n_in + 2]

    @pl.when(pl.program_id(2) == 0)
    def _():
        wb_ref[...] = w_ref[...].astype(BF16)

    acc = None
    off = 0
    for x_ref, ks in zip(refs[:n_in], k_sizes):
        if w_transposed:
            part = lax.dot_general(x_ref[...], wb_ref[:, off:off + ks], (((1,), (1,)), ((), ())),
                                   preferred_element_type=F32)
        else:
            part = jnp.dot(x_ref[...], wb_ref[off:off + ks, :], preferred_element_type=F32)
        acc = part if acc is None else acc + part
        off += ks
    o_ref[...] = acc.astype(o_ref.dtype)


def _matmul(xs, w_stack, layer, tm, tn, n_cols=None, k_chunks=1, out_dtype=F32,
            w_transposed=False, name="matmul"):
    m = xs[0].shape[0]
    if w_transposed:
        _, n, k = w_stack.shape
    else:
        _, k, n = w_stack.shape
    n_cols = n if n_cols is None else n_cols
    k_sizes = tuple(x.shape[1] for x in xs)
    assert m % tm == 0 and n_cols % tn == 0 and k % k_chunks == 0
    tk = k // k_chunks
    if k_chunks == 1:
        assert sum(k_sizes) == k
        in_specs = [pl.BlockSpec((tm, ks), lambda j, c, i: (i, 0)) for ks in k_sizes]
        out_specs = pl.BlockSpec((tm, tn), lambda j, c, i: (i, j))
        out_shape = jax.ShapeDtypeStruct((m, n_cols), out_dtype)
    else:
        assert k_sizes == (k,)
        k_sizes = (tk,)
        in_specs = [pl.BlockSpec((tm, tk), lambda j, c, i: (i, c))]
        out_specs = pl.BlockSpec((None, tm, tn), lambda j, c, i: (c, i, j))
        out_shape = jax.ShapeDtypeStruct((k_chunks, m, n_cols), out_dtype)
    if w_transposed:
        w_block = (tn, tk)
        in_specs.append(pl.BlockSpec((None, tn, tk), lambda j, c, i: (layer, j, c)))
    else:
        w_block = (tk, tn)
        in_specs.append(pl.BlockSpec((None, tk, tn), lambda j, c, i: (layer, c, j)))
    return pl.pallas_call(
        functools.partial(_mm_kernel, k_sizes=k_sizes, w_transposed=w_transposed),
        grid=(n_cols // tn, k_chunks, m // tm),
        in_specs=in_specs,
        out_specs=out_specs,
        out_shape=out_shape,
        scratch_shapes=[pltpu.VMEM(w_block, BF16)],
        compiler_params=_params("arbitrary", "arbitrary", "arbitrary"),
        name=name,
    )(*xs, w_stack)


def _ffn_up_kernel(x_ref, wg_ref, wu_ref, o_ref, wgb_ref, wub_ref):
    @pl.when(pl.program_id(1) == 0)
    def _():
        wgb_ref[...] = wg_ref[...].astype(BF16)
        wub_ref[...] = wu_ref[...].astype(BF16)

    x = x_ref[...]
    g = jnp.dot(x, wgb_ref[...], preferred_element_type=F32)
    u = jnp.dot(x, wub_ref[...], preferred_element_type=F32)
    o_ref[...] = (g * jax.nn.sigmoid(g) * u).astype(o_ref.dtype)


def _ffn_up(h, wg_stack, wu_stack, layer, tm, tn):
    m, k = h.shape
    n = wg_stack.shape[-1]
    assert m % tm == 0 and n % tn == 0
    wspec = pl.BlockSpec((None, k, tn), lambda j, i: (layer, 0, j))
    return pl.pallas_call(
        _ffn_up_kernel,
        grid=(n // tn, m // tm),
        in_specs=[pl.BlockSpec((tm, k), lambda j, i: (i, 0)), wspec, wspec],
        out_specs=pl.BlockSpec((tm, tn), lambda j, i: (i, j)),
        out_shape=jax.ShapeDtypeStruct((m, n), BF16),
        scratch_shapes=[pltpu.VMEM((k, tn), BF16), pltpu.VMEM((k, tn), BF16)],
        compiler_params=_params("arbitrary", "arbitrary"),
        name="ffn_up",
    )(h, wg_stack, wu_stack)


def _wstream_kernel(*refs, k_sizes, n_w, layer, w_transposed, tn, n_total, rc, n_chunks, n_blocks,
                    swiglu):
    n_x = len(k_sizes)
    x_refs = refs[:n_x]
    w_hbm = refs[n_x:n_x + n_w]
    o_ref = refs[n_x + n_w]
    wb = refs[n_x + n_w + 1:n_x + 2 * n_w + 1]
    st = refs[n_x + 2 * n_w + 1:n_x + 3 * n_w + 1]
    sem = refs[n_x + 3 * n_w + 1]
    j, i = pl.program_id(0), pl.program_id(1)
    q = j * n_chunks + i
    slot = j % 2
    shift = (tn - n_total % tn) % tn

    def col0(blk):
        return pl.multiple_of(jnp.minimum(blk * tn, n_total - tn), LANE)

    def chunk_copy(w, blk, ch, buf):
        r0 = pl.multiple_of(ch * rc, SUBLANE)
        if w_transposed:
            src = w_hbm[w].at[layer, pl.ds(pl.multiple_of(col0(blk) + r0, SUBLANE), rc), :]
        else:
            src = w_hbm[w].at[layer, pl.ds(r0, rc), pl.ds(col0(blk), tn)]
        return pltpu.make_async_copy(src, st[w].at[buf], sem.at[w, buf])

    def cast_chunk(w, ch, buf, dst_slot):
        r0 = pl.multiple_of(ch * rc, 2 * SUBLANE)
        wb[w][dst_slot, pl.ds(r0, rc), :] = st[w][buf].astype(BF16)

    @pl.when(q == 0)
    def _():
        for w in range(n_w):
            chunk_copy(w, 0, 0, 0).start()
        for ch in range(n_chunks):
            buf = ch % 2
            for w in range(n_w):
                if ch + 1 < n_chunks:
                    chunk_copy(w, 0, ch + 1, 1 - buf).start()
                chunk_copy(w, 0, ch, buf).wait()
                cast_chunk(w, ch, buf, 0)
        for w in range(n_w):
            chunk_copy(w, 0, n_chunks - 1, 1).start()

    nxt = (j + 1) % n_blocks
    pbuf = (q + 1) % 2
    pch = (i + n_chunks - 1) % n_chunks
    pblk = jnp.where(i >= 1, nxt, j)
    pslot = jnp.where(i >= 1, 1 - slot, slot)
    for w in range(n_w):
        chunk_copy(w, pblk, pch, pbuf).wait()
    for w in range(n_w):
        chunk_copy(w, nxt, i, q % 2).start()
    for w in range(n_w):
        cast_chunk(w, pch, pbuf, pslot)

    def product(w):
        acc = None
        off = 0
        for x_ref, ks in zip(x_refs, k_sizes):
            if w_transposed:
                part = lax.dot_general(x_ref[...], wb[w][slot, :, off:off + ks],
                                       (((1,), (1,)), ((), ())), preferred_element_type=F32)
            else:
                part = jnp.dot(x_ref[...], wb[w][slot, off:off + ks, :], preferred_element_type=F32)
            acc = part if acc is None else acc + part
            off += ks
        return acc

    out = product(0)
    if swiglu:
        out = out * jax.nn.sigmoid(out) * product(1)
    out = out.astype(o_ref.dtype)
    if shift == 0:
        o_ref[...] = out
    else:
        @pl.when(j != n_blocks - 1)
        def _():
            o_ref[...] = out

        @pl.when(j == n_blocks - 1)
        def _():
            o_ref[...] = jnp.concatenate([out[:, shift:], out[:, :shift]], axis=1)

    @pl.when(q == n_blocks * n_chunks - 1)
    def _():
        for w in range(n_w):
            chunk_copy(w, nxt, i, q % 2).wait()


def _matmul_wstream(xs, w_stacks, layer, tm, tn, n_cols=None, k_sizes=None, out_dtype=F32,
                    w_transposed=False, swiglu=False, name="matmul"):
    m = xs[0].shape[0]
    k_sizes = tuple(x.shape[1] for x in xs) if k_sizes is None else tuple(k_sizes)
    k = sum(k_sizes)
    if w_transposed:
        _, n, kw = w_stacks[0].shape
        rows, cols = tn, k
    else:
        _, kw, n = w_stacks[0].shape
        rows, cols = k, tn
    n_cols = n if n_cols is None else n_cols
    n_chunks = m // tm
    rc = rows // n_chunks
    n_blocks = -(-n_cols // tn)
    n_w = len(w_stacks)
    assert kw == k and m % tm == 0 and rows % n_chunks == 0 and rc % (2 * SUBLANE) == 0
    assert n_blocks >= 2 and n_cols >= tn and (not w_transposed or n_cols % tn == 0)
    assert n_w == (2 if swiglu else 1)
    in_specs = [pl.BlockSpec((tm, ks), lambda j, i: (i, 0)) for ks in k_sizes]
    in_specs += [pl.BlockSpec(memory_space=pl.ANY)] * n_w
    scratch = [pltpu.VMEM((2, rows, cols), BF16)] * n_w
    scratch += [pltpu.VMEM((2, rc, cols), F32)] * n_w
    scratch.append(pltpu.SemaphoreType.DMA((n_w, 2)))
    return pl.pallas_call(
        functools.partial(_wstream_kernel, k_sizes=k_sizes, n_w=n_w, layer=layer,
                          w_transposed=w_transposed, tn=tn, n_total=n_cols, rc=rc,
                          n_chunks=n_chunks, n_blocks=n_blocks, swiglu=swiglu),
        grid=(n_blocks, n_chunks),
        in_specs=in_specs,
        out_specs=pl.BlockSpec((tm, tn), lambda j, i: (i, j)),
        out_shape=jax.ShapeDtypeStruct((m, n_blocks * tn), out_dtype),
        scratch_shapes=scratch,
        compiler_params=_params("arbitrary", "arbitrary"),
        name=name,
    )(*xs, *w_stacks)


def _resid_norm_kernel(x_ref, y_ref, gp_ref, gn_ref, xo_ref, ho_ref, *, norm_next):
    y = y_ref[0]
    for c in range(1, y_ref.shape[0]):
        y = y + y_ref[c]
    xn = x_ref[...] + _rms(y, gp_ref[...])
    xo_ref[...] = xn
    if norm_next:
        ho_ref[...] = _rms(xn, gn_ref[...]).astype(ho_ref.dtype)
    else:
        ho_ref[...] = xn.astype(ho_ref.dtype)


def _resid_norm(x, y_parts, g_post, g_next):
    t, d = x.shape
    parts = y_parts.shape[0]
    tm = 256
    norm_next = g_next is not None
    if g_next is None:
        g_next = g_post
    big = pl.BlockSpec((tm, d), lambda i: (i, 0))
    row = pl.BlockSpec((1, d), lambda i: (0, 0))
    return pl.pallas_call(
        functools.partial(_resid_norm_kernel, norm_next=norm_next),
        grid=(t // tm,),
        in_specs=[big, pl.BlockSpec((parts, tm, d), lambda i: (0, i, 0)), row, row],
        out_specs=[big, big],
        out_shape=[jax.ShapeDtypeStruct((t, d), F32), jax.ShapeDtypeStruct((t, d), BF16)],
        compiler_params=_params("parallel"),
        name="resid_norm",
    )(x, y_parts, g_post[None, :], g_next[None, :])


def _ple_kernel(x_ref, gate_ref, p_ref, wp_ref, pn_ref, b_ref, gn_ref, xo_ref, *maybe_ho_ref):
    e = _rms(jnp.dot(p_ref[...].astype(BF16), wp_ref[...], preferred_element_type=F32), pn_ref[...])
    xn = x_ref[...] + jax.nn.sigmoid(gate_ref[...] + b_ref[...]) * e
    xo_ref[...] = xn
    for ho_ref in maybe_ho_ref:
        ho_ref[...] = _rms(xn, gn_ref[...]).astype(ho_ref.dtype)


def _ple(x, gate, p, w_ple, ple_norm, b_gate, g_next):
    t, d = x.shape
    tm = 256
    norm_next = g_next is not None
    if g_next is None:
        g_next = ple_norm
    big = pl.BlockSpec((tm, d), lambda i: (i, 0))
    row = pl.BlockSpec((1, d), lambda i: (0, 0))
    out_specs = [big, big] if norm_next else [big]
    out_shape = [jax.ShapeDtypeStruct((t, d), F32)]
    if norm_next:
        out_shape.append(jax.ShapeDtypeStruct((t, d), BF16))
    outs = pl.pallas_call(
        _ple_kernel,
        grid=(t // tm,),
        in_specs=[big, big, pl.BlockSpec((tm, PLE_DIM), lambda i: (i, 0)),
                  pl.BlockSpec((PLE_DIM, d), lambda i: (0, 0)), row, row, row],
        out_specs=out_specs,
        out_shape=out_shape,
        compiler_params=_params("parallel"),
        name="ple",
    )(x, gate, p, w_ple, ple_norm[None, :], b_gate[None, :], g_next[None, :])
    return (outs[0], outs[1]) if norm_next else (outs[0], None)


def _swa_kernel(sink_ref, q_ref, kc_ref, vc_ref, vp_ref, cos_ref, sin_ref, gn_ref, o_ref,
                kprev_ref):
    blk = SWA_BLOCK
    n = pl.program_id(1)
    cos, sin = cos_ref[...], sin_ref[...]

    @pl.when(n == 0)
    def _():
        kprev_ref[...] = jnp.zeros_like(kprev_ref)

    def rope(x):
        return x * cos + pltpu.roll(x, HEAD_DIM // 2, axis=1) * sin

    row = lax.broadcasted_iota(jnp.int32, (blk, 2 * blk), 0)
    col = lax.broadcasted_iota(jnp.int32, (blk, 2 * blk), 1)
    dist = blk + row - col
    kpos = (n - 1) * blk + col
    valid = (dist >= 0) & (dist < blk) & (kpos >= 0)
    scale = HEAD_DIM ** -0.5

    outs = []
    sq = jnp.zeros((blk, HEAD_DIM), F32)
    for kv in range(SWA_KV_HEADS):
        ks = slice(kv * HEAD_DIM, (kv + 1) * HEAD_DIM)
        k_cur = rope(kc_ref[:, ks]).astype(BF16)
        k = jnp.concatenate([kprev_ref[kv], k_cur], axis=0)
        kprev_ref[kv] = k_cur
        v = jnp.concatenate([vp_ref[:, ks], vc_ref[:, ks]], axis=0).astype(BF16)
        for g in range(SWA_GROUP):
            h = kv * SWA_GROUP + g
            q = rope(q_ref[:, h * HEAD_DIM:(h + 1) * HEAD_DIM]).astype(BF16)
            s = lax.dot_general(q, k, (((1,), (1,)), ((), ())), preferred_element_type=F32) * scale
            s = jnp.where(valid, s, NEG_INF)
            sink = sink_ref[h]
            m = jnp.maximum(jnp.max(s, axis=-1, keepdims=True), sink)
            p = jnp.exp(s - m)
            denom = jnp.sum(p[:, :blk] + p[:, blk:], axis=-1, keepdims=True) + jnp.exp(sink - m)
            o = jnp.dot(p.astype(BF16), v, preferred_element_type=F32) / denom
            outs.append(o)
            sq = sq + o * o
    inv = lax.rsqrt(jnp.sum(sq, axis=-1, keepdims=True) / SWA_OUT + NORM_EPS)
    for h, o in enumerate(outs):
        hs = slice(h * HEAD_DIM, (h + 1) * HEAD_DIM)
        o_ref[:, hs] = (o * inv * gn_ref[:, hs]).astype(o_ref.dtype)


def _swa(z3, cos, sin, sinks, gn):
    b, s, _ = z3.shape
    blk = SWA_BLOCK
    nb = s // blk
    kcol = SWA_OUT // SWA_KV_OUT
    vcol = kcol + 1
    cur = lambda bi, n: (bi, n, 0)
    tab = pl.BlockSpec((None, blk, LANE), cur)
    return pl.pallas_call(
        _swa_kernel,
        grid=(b, nb),
        in_specs=[
            pl.BlockSpec(memory_space=pltpu.SMEM),
            pl.BlockSpec((None, blk, SWA_OUT), cur),
            pl.BlockSpec((None, blk, SWA_KV_OUT), lambda bi, n: (bi, n, kcol)),
            pl.BlockSpec((None, blk, SWA_KV_OUT), lambda bi, n: (bi, n, vcol)),
            pl.BlockSpec((None, blk, SWA_KV_OUT), lambda bi, n: (bi, jnp.maximum(n - 1, 0), vcol)),
            tab, tab,
            pl.BlockSpec((1, SWA_OUT), lambda bi, n: (0, 0)),
        ],
        out_specs=pl.BlockSpec((None, blk, SWA_OUT), cur),
        out_shape=jax.ShapeDtypeStruct((b, s, SWA_OUT), BF16),
        scratch_shapes=[pltpu.VMEM((SWA_KV_HEADS, blk, HEAD_DIM), BF16)],
        compiler_params=_params("arbitrary", "arbitrary"),
        name="swa",
    )(sinks, z3, z3, z3, z3, cos, sin, gn[None, :])


RG_CHUNK = 256


def _rglru_kernel(x0_ref, x1_ref, g0_ref, g1_ref, cw_ref, cb_ref, wa_ref, ba_ref, wx_ref, bx_ref,
                  lam_ref, gn_ref, o_ref, xe_ref, a_ref, b_ref, h_ref, carry_ref):
    lc = RG_CHUNK
    pad = SUBLANE

    @pl.when(pl.program_id(1) == 0)
    def _():
        xe_ref[0:pad, :] = jnp.zeros((pad, RG_WIDTH), F32)
        carry_ref[...] = jnp.zeros_like(carry_ref)

    xe_ref[pad:pad + lc, :] = jnp.concatenate([x0_ref[...], x1_ref[...]], axis=1)
    cw = cw_ref[...]
    xc = cb_ref[...]
    for j in range(RG_CONV):
        sh = RG_CONV - 1 - j
        xc = xc + xe_ref[pad - sh:pad - sh + lc, :] * cw[j:j + 1, :]
    xe_ref[0:pad, :] = xe_ref[lc:lc + pad, :]

    xcb = xc.astype(BF16)

    def gate(w_ref, bias_ref):
        parts = [jnp.dot(xcb[:, nb * RG_BLOCK_DIM:(nb + 1) * RG_BLOCK_DIM], w_ref[nb],
                         preferred_element_type=F32) for nb in range(RG_BLOCKS)]
        return jax.nn.sigmoid(jnp.concatenate(parts, axis=1) + bias_ref[...])

    r = gate(wa_ref, ba_ref)
    i = gate(wx_ref, bx_ref)
    nlam = -lam_ref[...]
    softplus = jnp.maximum(nlam, 0.0) + jnp.log1p(jnp.exp(-jnp.abs(nlam)))
    log_a = (-RG_C * softplus) * r
    a = jnp.exp(log_a)
    one_minus_a2 = -jnp.tanh(log_a) * (a * a + 1.0)
    a_ref[...] = a
    b_ref[...] = jnp.sqrt(one_minus_a2) * i * xc

    sub = lax.broadcasted_iota(jnp.int32, (SUBLANE, RG_WIDTH), 0)

    def scan_tile(t, h):
        r0 = pl.multiple_of(t * SUBLANE, SUBLANE)
        at = a_ref[pl.ds(r0, SUBLANE), :]
        bt = b_ref[pl.ds(r0, SUBLANE), :]
        for sh in (1, 2, 4):
            keep = sub >= sh
            a_sh = pltpu.roll(at, sh, axis=0)
            b_sh = pltpu.roll(bt, sh, axis=0)
            bt = jnp.where(keep, at * b_sh + bt, bt)
            at = jnp.where(keep, at * a_sh, at)
        ht = at * h + bt
        h_ref[pl.ds(r0, SUBLANE), :] = ht
        return ht[SUBLANE - 1:SUBLANE, :]

    carry_ref[0:1, :] = lax.fori_loop(0, lc // SUBLANE, scan_tile, carry_ref[0:1, :], unroll=4)

    g = jnp.concatenate([g0_ref[...], g1_ref[...]], axis=1)
    y = h_ref[...] * jax.nn.gelu(g)
    o_ref[...] = _rms(y, gn_ref[...]).astype(o_ref.dtype)


def _rglru(z3, conv_w, conv_b, wa, ba, wx, bx, lam, gn):
    b, s, _ = z3.shape
    lc = RG_CHUNK
    half = RG_WIDTH // 2
    x_col = (SWA_OUT + 2 * SWA_KV_OUT) // half
    g_col = x_col + 2

    def zcol(c):
        return pl.BlockSpec((None, lc, half), lambda bi, ci: (bi, ci, c))

    row = pl.BlockSpec((1, RG_WIDTH), lambda bi, ci: (0, 0))
    wspec = pl.BlockSpec((RG_BLOCKS, RG_BLOCK_DIM, RG_BLOCK_DIM), lambda bi, ci: (0, 0, 0))
    return pl.pallas_call(
        _rglru_kernel,
        grid=(b, s // lc),
        in_specs=[zcol(x_col), zcol(x_col + 1), zcol(g_col), zcol(g_col + 1),
                  pl.BlockSpec((RG_CONV, RG_WIDTH), lambda bi, ci: (0, 0)), row,
                  wspec, row, wspec, row, row, row],
        out_specs=pl.BlockSpec((None, lc, RG_WIDTH), lambda bi, ci: (bi, ci, 0)),
        out_shape=jax.ShapeDtypeStruct((b, s, RG_WIDTH), BF16),
        scratch_shapes=[pltpu.VMEM((lc + SUBLANE, RG_WIDTH), F32),
                        pltpu.VMEM((lc, RG_WIDTH), F32),
                        pltpu.VMEM((lc, RG_WIDTH), F32),
                        pltpu.VMEM((lc, RG_WIDTH), F32),
                        pltpu.VMEM((SUBLANE, RG_WIDTH), F32)],
        compiler_params=_params("parallel", "arbitrary"),
        name="rglru",
    )(z3, z3, z3, z3, conv_w, conv_b[None, :], wa, ba[None, :], wx, bx[None, :],
      lam[None, :], gn[None, :])


def _rope64(x, cos, sin, first_half):
    swapped = jnp.where(first_half, pltpu.roll(x, LANE - MLA_ROPE // 2, axis=1),
                        pltpu.roll(x, MLA_ROPE // 2, axis=1))
    return x * cos + swapped * sin


def _mla_proj_kernel(cq0_ref, cq1_ref, ckv_ref, kr_ref, cos_ref, sin_ref, qn_ref, kvn_ref,
                     wqn_ref, wqr_ref, wkn_ref, wv_ref, q_out, kn_out, kr_out, v_out):
    cos, sin = cos_ref[...], sin_ref[...]
    tm = cos.shape[0]
    lane = lax.broadcasted_iota(jnp.int32, (tm, LANE), 1)
    first_half = (lane % MLA_ROPE) < (MLA_ROPE // 2)
    low = lane < MLA_ROPE

    hq = _rms(jnp.concatenate([cq0_ref[...], cq1_ref[...]], axis=1), qn_ref[...]).astype(BF16)
    qn = jnp.dot(hq, wqn_ref[...], preferred_element_type=F32) * MLA_Q_SCALE
    qr = jnp.dot(hq, wqr_ref[...], preferred_element_type=F32) * MLA_Q_SCALE
    for pair in range(MLA_HEADS // 2):
        rot = _rope64(qr[:, pair * LANE:(pair + 1) * LANE], cos, sin, first_half)
        for odd in range(2):
            h = 2 * pair + odd
            piece = pltpu.roll(rot, MLA_ROPE, axis=1) if odd else rot
            q_out[h, :, 0:MLA_NOPE] = qn[:, h * MLA_NOPE:(h + 1) * MLA_NOPE].astype(BF16)
            q_out[h, :, MLA_NOPE:MLA_QK_PAD] = jnp.where(low, piece, 0.0).astype(BF16)

    hkv = _rms(ckv_ref[...], kvn_ref[...]).astype(BF16)
    kn = jnp.dot(hkv, wkn_ref[...], preferred_element_type=F32)
    v = jnp.dot(hkv, wv_ref[...], preferred_element_type=F32)
    for h in range(MLA_HEADS):
        hs = slice(h * LANE, (h + 1) * LANE)
        kn_out[h] = kn[:, hs].astype(BF16)
        v_out[h] = v[:, hs].astype(BF16)
    kr_out[...] = _rope64(kr_ref[...], cos, sin, first_half).astype(BF16)


def _mla_proj(z, kr, cos64, sin64, q_norm, kv_norm, wqn, wqr, wkn, wv, b, s):
    t = z.shape[0]
    tm = 512
    per_b = s // tm
    half = MLA_Q_RANK // 2
    cq_col = (SWA_OUT + 2 * SWA_KV_OUT + 2 * RG_WIDTH) // half
    tok = lambda i: (i, 0)
    head_blk = lambda i: (i // per_b, 0, i % per_b, 0)
    full = lambda shape: pl.BlockSpec(shape, lambda i: (0, 0))
    return pl.pallas_call(
        _mla_proj_kernel,
        grid=(t // tm,),
        in_specs=[pl.BlockSpec((tm, half), lambda i: (i, cq_col)),
                  pl.BlockSpec((tm, half), lambda i: (i, cq_col + 1)),
                  pl.BlockSpec((tm, MLA_KV_RANK), lambda i: (i, cq_col + 2)),
                  pl.BlockSpec((tm, LANE), tok), pl.BlockSpec((tm, LANE), tok),
                  pl.BlockSpec((tm, LANE), tok),
                  full((1, MLA_Q_RANK)), full((1, MLA_KV_RANK)),
                  full(wqn.shape), full(wqr.shape), full(wkn.shape), full(wv.shape)],
        out_specs=[pl.BlockSpec((None, MLA_HEADS, tm, MLA_QK_PAD), head_blk),
                   pl.BlockSpec((None, MLA_HEADS, tm, LANE), head_blk),
                   pl.BlockSpec((None, tm, LANE), lambda i: (i // per_b, i % per_b, 0)),
                   pl.BlockSpec((None, MLA_HEADS, tm, LANE), head_blk)],
        out_shape=[jax.ShapeDtypeStruct((b, MLA_HEADS, s, MLA_QK_PAD), BF16),
                   jax.ShapeDtypeStruct((b, MLA_HEADS, s, LANE), BF16),
                   jax.ShapeDtypeStruct((b, s, LANE), BF16),
                   jax.ShapeDtypeStruct((b, MLA_HEADS, s, LANE), BF16)],
        compiler_params=_params("parallel"),
        name="mla_proj",
    )(z, z, z, kr, cos64, sin64, q_norm[None, :], kv_norm[None, :], wqn, wqr, wkn, wv)


MLA_TQ = 512


def _mla_attn_kernel(qi_tab, kj_tab, q_ref, kn_ref, kr_ref, v_ref, gn_ref, o_ref,
                     m_ref, l_ref, acc_ref):
    tq = MLA_TQ
    step = pl.program_id(1)
    qi = qi_tab[step]
    kj = kj_tab[step]

    @pl.when(kj == 0)
    def _():
        m_ref[...] = jnp.full(m_ref.shape, NEG_INF, F32)
        l_ref[...] = jnp.zeros_like(l_ref)
        acc_ref[...] = jnp.zeros_like(acc_ref)

    kr = kr_ref[...]

    def attend(masked):
        if masked:
            row = lax.broadcasted_iota(jnp.int32, (tq, tq), 0)
            col = lax.broadcasted_iota(jnp.int32, (tq, tq), 1)
            causal = col <= row
        for h in range(MLA_HEADS):
            k = jnp.concatenate([kn_ref[h], kr], axis=1)
            s = lax.dot_general(q_ref[h], k, (((1,), (1,)), ((), ())),
                                preferred_element_type=F32)
            if masked:
                s = jnp.where(causal, s, NEG_INF)
            m_old = m_ref[h]
            m_new = jnp.maximum(m_old, jnp.max(s, axis=-1, keepdims=True))
            alpha = jnp.exp2(m_old - m_new)
            p = jnp.exp2(s - jnp.concatenate([m_new] * (tq // LANE), axis=1))
            l_ref[h] = alpha * l_ref[h] + sum(p[:, c:c + LANE] for c in range(0, tq, LANE))
            acc_ref[h] = alpha * acc_ref[h] + jnp.dot(p.astype(BF16), v_ref[h],
                                                      preferred_element_type=F32)
            m_ref[h] = m_new

    @pl.when(kj != qi)
    def _():
        attend(False)

    @pl.when(kj == qi)
    def _():
        attend(True)

    @pl.when(kj == qi)
    def _():
        sq = jnp.zeros((tq, MLA_V), F32)
        for h in range(MLA_HEADS):
            o = acc_ref[h] / jnp.sum(l_ref[h], axis=-1, keepdims=True)
            acc_ref[h] = o
            sq = sq + o * o
        inv = lax.rsqrt(jnp.sum(sq, axis=-1, keepdims=True) / MLA_OUT + NORM_EPS)
        for h in range(MLA_HEADS):
            hs = slice(h * MLA_V, (h + 1) * MLA_V)
            o_ref[:, hs] = (acc_ref[h] * inv * gn_ref[:, hs]).astype(o_ref.dtype)


def _mla_attn(q, kn, kr, v, gn):
    b, _, s, _ = q.shape
    tq = MLA_TQ
    nq = s // tq
    pairs = [(qi, kj) for qi in range(nq) for kj in range(qi + 1)]
    qi_tab = jnp.asarray(np.array([pr[0] for pr in pairs], np.int32))
    kj_tab = jnp.asarray(np.array([pr[1] for pr in pairs], np.int32))
    head_rows = lambda tab_of: (lambda bi, st, qt, kt: (bi, 0, tab_of(qt, kt)[st], 0))
    by_q = lambda qt, kt: qt
    by_k = lambda qt, kt: kt
    grid_spec = pltpu.PrefetchScalarGridSpec(
        num_scalar_prefetch=2,
        grid=(b, len(pairs)),
        in_specs=[pl.BlockSpec((None, MLA_HEADS, tq, MLA_QK_PAD), head_rows(by_q)),
                  pl.BlockSpec((None, MLA_HEADS, tq, LANE), head_rows(by_k)),
                  pl.BlockSpec((None, tq, LANE), lambda bi, st, qt, kt: (bi, kt[st], 0)),
                  pl.BlockSpec((None, MLA_HEADS, tq, LANE), head_rows(by_k)),
                  pl.BlockSpec((1, MLA_OUT), lambda bi, st, qt, kt: (0, 0))],
        out_specs=pl.BlockSpec((None, tq, MLA_OUT), lambda bi, st, qt, kt: (bi, qt[st], 0)),
        scratch_shapes=[pltpu.VMEM((MLA_HEADS, tq, LANE), F32),
                        pltpu.VMEM((MLA_HEADS, tq, LANE), F32),
                        pltpu.VMEM((MLA_HEADS, tq, MLA_V), F32)],
    )
    return pl.pallas_call(
        _mla_attn_kernel,
        grid_spec=grid_spec,
        out_shape=jax.ShapeDtypeStruct((b, s, MLA_OUT), BF16),
        compiler_params=_params("parallel", "arbitrary"),
        name="mla_attn",
    )(qi_tab, kj_tab, q, kn, kr, v, gn[None, :])


def kernel(x, p, positions, pre_mix_norm, w_in, swa_sinks, rg_conv_w, rg_conv_b, rg_gate_a_w,
           rg_gate_a_b, rg_gate_x_w, rg_gate_x_b, rg_lambda, mla_q_norm, mla_w_uq, mla_kv_norm,
           mla_w_ukv, group_norm, w_out, post_mix_norm, pre_ffn_norm, w_gate, w_up, w_down,
           post_ffn_norm, w_ple, ple_norm, w_ple_gate, b_ple_gate):
    b, s, d = x.shape
    t = b * s
    depth = w_in.shape[0]

    pos_col = positions.astype(F32).reshape(t, 1)
    cos128, sin128 = _rope_tables(pos_col, HEAD_DIM)
    cos64, sin64 = _rope_tables(pos_col, MLA_ROPE)
    cos128_3, sin128_3 = cos128.reshape(b, s, LANE), sin128.reshape(b, s, LANE)

    w_in_t = jnp.swapaxes(w_in, 1, 2)
    xf = x.reshape(t, d)
    h = _norm_cast(xf, pre_mix_norm[0])
    for i in range(depth):
        z = _matmul_wstream([h], [w_in_t], i, 1024, 1024, n_cols=Z_MAIN, w_transposed=True,
                            name="in_proj")
        w_kr = jnp.pad(w_in_t[i, Z_MAIN:, :], ((0, LANE - MLA_ROPE), (0, 0)))[None]
        kr = _matmul([h], w_kr, 0, 1024, LANE, w_transposed=True, name="in_proj_kr")
        z3 = z.reshape(b, s, Z_MAIN)
        gn = group_norm[i]

        o_a = _swa(z3, cos128_3, sin128_3, swa_sinks[i], gn[:SWA_OUT])
        o_b = _rglru(z3, rg_conv_w[i], rg_conv_b[i], rg_gate_a_w[i].astype(BF16), rg_gate_a_b[i],
                     rg_gate_x_w[i].astype(BF16), rg_gate_x_b[i], rg_lambda[i],
                     gn[SWA_OUT:SWA_OUT + RG_WIDTH])
        w_uq = mla_w_uq[i].astype(BF16).reshape(MLA_Q_RANK, MLA_HEADS, MLA_NOPE + MLA_ROPE)
        w_ukv = mla_w_ukv[i].astype(BF16).reshape(MLA_KV_RANK, MLA_HEADS, MLA_NOPE + MLA_V)
        q_c, kn_c, kr_c, v_c = _mla_proj(
            z, kr, cos64, sin64, mla_q_norm[i], mla_kv_norm[i],
            w_uq[:, :, :MLA_NOPE].reshape(MLA_Q_RANK, MLA_HEADS * MLA_NOPE),
            w_uq[:, :, MLA_NOPE:].reshape(MLA_Q_RANK, MLA_HEADS * MLA_ROPE),
            w_ukv[:, :, :MLA_NOPE].reshape(MLA_KV_RANK, MLA_HEADS * MLA_NOPE),
            w_ukv[:, :, MLA_NOPE:].reshape(MLA_KV_RANK, MLA_HEADS * MLA_V), b, s)
        o_c = _mla_attn(q_c, kn_c, kr_c, v_c, gn[SWA_OUT + RG_WIDTH:])

        mix = _matmul_wstream(
            [o_a.reshape(t, SWA_OUT), o_b.reshape(t, RG_WIDTH), o_c.reshape(t, MLA_OUT)],
            [w_out], i, 1024, 1024, name="out_proj")
        xf, h = _resid_norm(xf, mix[None], post_mix_norm[i], pre_ffn_norm[i])

        gu = _matmul_wstream([h], [w_gate, w_up], i, 1024, 512, out_dtype=BF16, swiglu=True,
                             name="ffn_up")
        f = _matmul_wstream([gu], [w_down], i, 512, 512, k_sizes=[w_down.shape[1]], name="ffn_down")
        xf, xb = _resid_norm(xf, f[None], post_ffn_norm[i], None)

        gate = _matmul_wstream([xb], [w_ple_gate], i, 1024, 1024, name="ple_gate")
        g_next = pre_mix_norm[i + 1] if i + 1 < depth else None
        xf, h = _ple(xf, gate, p[i].reshape(t, PLE_DIM), w_ple[i].astype(BF16), ple_norm[i],
                     b_ple_gate[i], g_next)
    return xf.reshape(b, s, d)
```

```python
import functools
import math

import jax
import jax.numpy as jnp
import numpy as np
from jax import lax
from jax.experimental import pallas as pl
from jax.experimental.pallas import tpu as pltpu

F32 = jnp.float32
BF16 = jnp.bfloat16

D_MODEL = 4096
HEAD_DIM = 128
ROPE_THETA = 10000.0
NORM_EPS = 1e-6
PLE_DIM = 256
NEG_INF = -1e30

SWA_HEADS = 12
SWA_KV_HEADS = 4
SWA_GROUP = SWA_HEADS // SWA_KV_HEADS
SWA_BLOCK = 128
SWA_OUT = SWA_HEADS * HEAD_DIM
SWA_KV_OUT = SWA_KV_HEADS * HEAD_DIM

RG_WIDTH = 1024
RG_BLOCKS = 8
RG_BLOCK_DIM = RG_WIDTH // RG_BLOCKS
RG_CONV = 4
RG_C = 8.0

MLA_HEADS = 12
MLA_Q_RANK = 1024
MLA_KV_RANK = 512
MLA_NOPE = 128
MLA_ROPE = 64
MLA_V = 128
MLA_OUT = MLA_HEADS * MLA_V
MLA_QK_PAD = 256
MLA_Q_SCALE = (MLA_NOPE + MLA_ROPE) ** -0.5 * math.log2(math.e)

MIX_WIDTH = SWA_OUT + RG_WIDTH + MLA_OUT
Z_MAIN = SWA_OUT + 2 * SWA_KV_OUT + 2 * RG_WIDTH + MLA_Q_RANK + MLA_KV_RANK

V7X_VMEM_LIMIT_BYTES = 56 * 1024 * 1024
LANE = 128
SUBLANE = 8


def _params(*semantics):
    return pltpu.CompilerParams(dimension_semantics=semantics,
                                vmem_limit_bytes=V7X_VMEM_LIMIT_BYTES)


def _rms(xf, g):
    ms = jnp.mean(xf * xf, axis=-1, keepdims=True)
    return xf * lax.rsqrt(ms + NORM_EPS) * g


def _rope_table_kernel(pos_ref, inv_ref, sign_ref, cos_ref, sin_ref):
    ang = pos_ref[...] * inv_ref[...]
    cos_ref[...] = jnp.cos(ang)
    sin_ref[...] = jnp.sin(ang) * sign_ref[...]


def _rope_tables(pos_col, rot_dim):
    t = pos_col.shape[0]
    half = rot_dim // 2
    inv = ROPE_THETA ** (-jnp.arange(0, rot_dim, 2, dtype=F32) / rot_dim)
    inv_row = jnp.tile(inv, LANE // half)[None, :]
    sign_row = jnp.tile(jnp.concatenate([-jnp.ones((half,), F32), jnp.ones((half,), F32)]),
                        LANE // rot_dim)[None, :]
    tm = 1024
    row = pl.BlockSpec((1, LANE), lambda i: (0, 0))
    return pl.pallas_call(
        _rope_table_kernel,
        grid=(t // tm,),
        in_specs=[pl.BlockSpec((tm, 1), lambda i: (i, 0)), row, row],
        out_specs=[pl.BlockSpec((tm, LANE), lambda i: (i, 0))] * 2,
        out_shape=[jax.ShapeDtypeStruct((t, LANE), F32)] * 2,
        compiler_params=_params("parallel"),
        name="rope_tables",
    )(pos_col, inv_row, sign_row)


def _norm_cast_kernel(x_ref, g_ref, o_ref):
    o_ref[...] = _rms(x_ref[...], g_ref[...]).astype(o_ref.dtype)


def _norm_cast(x, g):
    t, d = x.shape
    tm = 256
    return pl.pallas_call(
        _norm_cast_kernel,
        grid=(t // tm,),
        in_specs=[pl.BlockSpec((tm, d), lambda i: (i, 0)),
                  pl.BlockSpec((1, d), lambda i: (0, 0))],
        out_specs=pl.BlockSpec((tm, d), lambda i: (i, 0)),
        out_shape=jax.ShapeDtypeStruct((t, d), BF16),
        compiler_params=_params("parallel"),
        name="norm_cast",
    )(x, g[None, :])


def _mm_kernel(*refs, k_sizes, w_transposed):
    n_in = len(k_sizes)
    w_ref, o_ref, wb_ref = refs[n_in], refs[n_in + 1], refs[n_in + 2]

    @pl.when(pl.program_id(2) == 0)
    def _():
        wb_ref[...] = w_ref[...].astype(BF16)

    acc = None
    off = 0
    for x_ref, ks in zip(refs[:n_in], k_sizes):
        if w_transposed:
            part = lax.dot_general(x_ref[...], wb_ref[:, off:off + ks], (((1,), (1,)), ((), ())),
                                   preferred_element_type=F32)
        else:
            part = jnp.dot(x_ref[...], wb_ref[off:off + ks, :], preferred_element_type=F32)
        acc = part if acc is None else acc + part
        off += ks
    o_ref[...] = acc.astype(o_ref.dtype)


def _matmul(xs, w_stack, layer, tm, tn, n_cols=None, k_chunks=1, out_dtype=F32,
            w_transposed=False, name="matmul"):
    m = xs[0].shape[0]
    if w_transposed:
        _, n, k = w_stack.shape
    else:
        _, k, n = w_stack.shape
    n_cols = n if n_cols is None else n_cols
    k_sizes = tuple(x.shape[1] for x in xs)
    assert m % tm == 0 and n_cols % tn == 0 and k % k_chunks == 0
    tk = k // k_chunks
    if k_chunks == 1:
        assert sum(k_sizes) == k
        in_specs = [pl.BlockSpec((tm, ks), lambda j, c, i: (i, 0)) for ks in k_sizes]
        out_specs = pl.BlockSpec((tm, tn), lambda j, c, i: (i, j))
        out_shape = jax.ShapeDtypeStruct((m, n_cols), out_dtype)
    else:
        assert k_sizes == (k,)
        k_sizes = (tk,)
        in_specs = [pl.BlockSpec((tm, tk), lambda j, c, i: (i, c))]
        out_specs = pl.BlockSpec((None, tm, tn), lambda j, c, i: (c, i, j))
        out_shape = jax.ShapeDtypeStruct((k_chunks, m, n_cols), out_dtype)
    if w_transposed:
        w_block = (tn, tk)
        in_specs.append(pl.BlockSpec((None, tn, tk), lambda j, c, i: (layer, j, c)))
    else:
        w_block = (tk, tn)
        in_specs.append(pl.BlockSpec((None, tk, tn), lambda j, c, i: (layer, c, j)))
    return pl.pallas_call(
        functools.partial(_mm_kernel, k_sizes=k_sizes, w_transposed=w_transposed),
        grid=(n_cols // tn, k_chunks, m // tm),
        in_specs=in_specs,
        out_specs=out_specs,
        out_shape=out_shape,
        scratch_shapes=[pltpu.VMEM(w_block, BF16)],
        compiler_params=_params("arbitrary", "arbitrary", "arbitrary"),
        name=name,
    )(*xs, w_stack)


def _ffn_up_kernel(x_ref, wg_ref, wu_ref, o_ref, wgb_ref, wub_ref):
    @pl.when(pl.program_id(1) == 0)
    def _():
        wgb_ref[...] = wg_ref[...].astype(BF16)
        wub_ref[...] = wu_ref[...].astype(BF16)

    x = x_ref[...]
    g = jnp.dot(x, wgb_ref[...], preferred_element_type=F32)
    u = jnp.dot(x, wub_ref[...], preferred_element_type=F32)
    o_ref[...] = (g * jax.nn.sigmoid(g) * u).astype(o_ref.dtype)


def _ffn_up(h, wg_stack, wu_stack, layer, tm, tn):
    m, k = h.shape
    n = wg_stack.shape[-1]
    assert m % tm == 0 and n % tn == 0
    wspec = pl.BlockSpec((None, k, tn), lambda j, i: (layer, 0, j))
    return pl.pallas_call(
        _ffn_up_kernel,
        grid=(n // tn, m // tm),
        in_specs=[pl.BlockSpec((tm, k), lambda j, i: (i, 0)), wspec, wspec],
        out_specs=pl.BlockSpec((tm, tn), lambda j, i: (i, j)),
        out_shape=jax.ShapeDtypeStruct((m, n), BF16),
        scratch_shapes=[pltpu.VMEM((k, tn), BF16), pltpu.VMEM((k, tn), BF16)],
        compiler_params=_params("arbitrary", "arbitrary"),
        name="ffn_up",
    )(h, wg_stack, wu_stack)


def _wstream_kernel(*refs, k_sizes, n_w, layer, w_transposed, tn, n_total, rc, n_chunks, n_blocks,
                    swiglu):
    n_x = len(k_sizes)
    x_refs = refs[:n_x]
    w_hbm = refs[n_x:n_x + n_w]
    o_ref = refs[n_x + n_w]
    wb = refs[n_x + n_w + 1:n_x + 2 * n_w + 1]
    st = refs[n_x + 2 * n_w + 1:n_x + 3 * n_w + 1]
    sem = refs[n_x + 3 * n_w + 1]
    j, i = pl.program_id(0), pl.program_id(1)
    q = j * n_chunks + i
    slot = j % 2
    shift = (tn - n_total % tn) % tn

    def col0(blk):
        return pl.multiple_of(jnp.minimum(blk * tn, n_total - tn), LANE)

    def chunk_copy(w, blk, ch, buf):
        r0 = pl.multiple_of(ch * rc, SUBLANE)
        if w_transposed:
            src = w_hbm[w].at[layer, pl.ds(pl.multiple_of(col0(blk) + r0, SUBLANE), rc), :]
        else:
            src = w_hbm[w].at[layer, pl.ds(r0, rc), pl.ds(col0(blk), tn)]
        return pltpu.make_async_copy(src, st[w].at[buf], sem.at[w, buf])

    def cast_chunk(w, ch, buf, dst_slot):
        r0 = pl.multiple_of(ch * rc, 2 * SUBLANE)
        wb[w][dst_slot, pl.ds(r0, rc), :] = st[w][buf].astype(BF16)

    @pl.when(q == 0)
    def _():
        for w in range(n_w):
            chunk_copy(w, 0, 0, 0).start()
        for ch in range(n_chunks):
            buf = ch % 2
            for w in range(n_w):
                if ch + 1 < n_chunks:
                    chunk_copy(w, 0, ch + 1, 1 - buf).start()
                chunk_copy(w, 0, ch, buf).wait()
                cast_chunk(w, ch, buf, 0)
        for w in range(n_w):
            chunk_copy(w, 0, n_chunks - 1, 1).start()

    nxt = (j + 1) % n_blocks
    pbuf = (q + 1) % 2
    pch = (i + n_chunks - 1) % n_chunks
    pblk = jnp.where(i >= 1, nxt, j)
    pslot = jnp.where(i >= 1, 1 - slot, slot)
    for w in range(n_w):
        chunk_copy(w, pblk, pch, pbuf).wait()
    for w in range(n_w):
        chunk_copy(w, nxt, i, q % 2).start()
    for w in range(n_w):
        cast_chunk(w, pch, pbuf, pslot)

    def product(w):
        acc = None
        off = 0
        for x_ref, ks in zip(x_refs, k_sizes):
            if w_transposed:
                part = lax.dot_general(x_ref[...], wb[w][slot, :, off:off + ks],
                                       (((1,), (1,)), ((), ())), preferred_element_type=F32)
            else:
                part = jnp.dot(x_ref[...], wb[w][slot, off:off + ks, :], preferred_element_type=F32)
            acc = part if acc is None else acc + part
            off += ks
        return acc

    out = product(0)
    if swiglu:
        out = out * jax.nn.sigmoid(out) * product(1)
    out = out.astype(o_ref.dtype)
    if shift == 0:
        o_ref[...] = out
    else:
        @pl.when(j != n_blocks - 1)
        def _():
            o_ref[...] = out

        @pl.when(j == n_blocks - 1)
        def _():
            o_ref[...] = jnp.concatenate([out[:, shift:], out[:, :shift]], axis=1)

    @pl.when(q == n_blocks * n_chunks - 1)
    def _():
        for w in range(n_w):
            chunk_copy(w, nxt, i, q % 2).wait()


def _matmul_wstream(xs, w_stacks, layer, tm, tn, n_cols=None, k_sizes=None, out_dtype=F32,
                    w_transposed=False, swiglu=False, name="matmul"):
    m = xs[0].shape[0]
    k_sizes = tuple(x.shape[1] for x in xs) if k_sizes is None else tuple(k_sizes)
    k = sum(k_sizes)
    if w_transposed:
        _, n, kw = w_stacks[0].shape
        rows, cols = tn, k
    else:
        _, kw, n = w_stacks[0].shape
        rows, cols = k, tn
    n_cols = n if n_cols is None else n_cols
    n_chunks = m // tm
    rc = rows // n_chunks
    n_blocks = -(-n_cols // tn)
    n_w = len(w_stacks)
    assert kw == k and m % tm == 0 and rows % n_chunks == 0 and rc % (2 * SUBLANE) == 0
    assert n_blocks >= 2 and n_cols >= tn and (not w_transposed or n_cols % tn == 0)
    assert n_w == (2 if swiglu else 1)
    in_specs = [pl.BlockSpec((tm, ks), lambda j, i: (i, 0)) for ks in k_sizes]
    in_specs += [pl.BlockSpec(memory_space=pl.ANY)] * n_w
    scratch = [pltpu.VMEM((2, rows, cols), BF16)] * n_w
    scratch += [pltpu.VMEM((2, rc, cols), F32)] * n_w
    scratch.append(pltpu.SemaphoreType.DMA((n_w, 2)))
    return pl.pallas_call(
        functools.partial(_wstream_kernel, k_sizes=k_sizes, n_w=n_w, layer=layer,
                          w_transposed=w_transposed, tn=tn, n_total=n_cols, rc=rc,
                          n_chunks=n_chunks, n_blocks=n_blocks, swiglu=swiglu),
        grid=(n_blocks, n_chunks),
        in_specs=in_specs,
        out_specs=pl.BlockSpec((tm, tn), lambda j, i: (i, j)),
        out_shape=jax.ShapeDtypeStruct((m, n_blocks * tn), out_dtype),
        scratch_shapes=scratch,
        compiler_params=_params("arbitrary", "arbitrary"),
        name=name,
    )(*xs, *w_stacks)


def _resid_norm_kernel(x_ref, y_ref, gp_ref, gn_ref, xo_ref, ho_ref, *, norm_next):
    y = y_ref[0]
    for c in range(1, y_ref.shape[0]):
        y = y + y_ref[c]
    xn = x_ref[...] + _rms(y, gp_ref[...])
    xo_ref[...] = xn
    if norm_next:
        ho_ref[...] = _rms(xn, gn_ref[...]).astype(ho_ref.dtype)
    else:
        ho_ref[...] = xn.astype(ho_ref.dtype)


def _resid_norm(x, y_parts, g_post, g_next):
    t, d = x.shape
    parts = y_parts.shape[0]
    tm = 256
    norm_next = g_next is not None
    if g_next is None:
        g_next = g_post
    big = pl.BlockSpec((tm, d), lambda i: (i, 0))
    row = pl.BlockSpec((1, d), lambda i: (0, 0))
    return pl.pallas_call(
        functools.partial(_resid_norm_kernel, norm_next=norm_next),
        grid=(t // tm,),
        in_specs=[big, pl.BlockSpec((parts, tm, d), lambda i: (0, i, 0)), row, row],
        out_specs=[big, big],
        out_shape=[jax.ShapeDtypeStruct((t, d), F32), jax.ShapeDtypeStruct((t, d), BF16)],
        compiler_params=_params("parallel"),
        name="resid_norm",
    )(x, y_parts, g_post[None, :], g_next[None, :])


def _ple_kernel(x_ref, gate_ref, p_ref, wp_ref, pn_ref, b_ref, gn_ref, xo_ref, *maybe_ho_ref):
    e = _rms(jnp.dot(p_ref[...].astype(BF16), wp_ref[...], preferred_element_type=F32), pn_ref[...])
    xn = x_ref[...] + jax.nn.sigmoid(gate_ref[...] + b_ref[...]) * e
    xo_ref[...] = xn
    for ho_ref in maybe_ho_ref:
        ho_ref[...] = _rms(xn, gn_ref[...]).astype(ho_ref.dtype)


def _ple(x, gate, p, w_ple, ple_norm, b_gate, g_next):
    t, d = x.shape
    tm = 256
    norm_next = g_next is not None
    if g_next is None:
        g_next = ple_norm
    big = pl.BlockSpec((tm, d), lambda i: (i, 0))
    row = pl.BlockSpec((1, d), lambda i: (0, 0))
    out_specs = [big, big] if norm_next else [big]
    out_shape = [jax.ShapeDtypeStruct((t, d), F32)]
    if norm_next:
        out_shape.append(jax.ShapeDtypeStruct((t, d), BF16))
    outs = pl.pallas_call(
        _ple_kernel,
        grid=(t // tm,),
        in_specs=[big, big, pl.BlockSpec((tm, PLE_DIM), lambda i: (i, 0)),
                  pl.BlockSpec((PLE_DIM, d), lambda i: (0, 0)), row, row, row],
        out_specs=out_specs,
        out_shape=out_shape,
        compiler_params=_params("parallel"),
        name="ple",
    )(x, gate, p, w_ple, ple_norm[None, :], b_gate[None, :], g_next[None, :])
    return (outs[0], outs[1]) if norm_next else (outs[0], None)


def _swa_kernel(sink_ref, q_ref, kc_ref, vc_ref, vp_ref, cos_ref, sin_ref, gn_ref, o_ref,
                kprev_ref):
    blk = SWA_BLOCK
    n = pl.program_id(1)
    cos, sin = cos_ref[...], sin_ref[...]

    @pl.when(n == 0)
    def _():
        kprev_ref[...] = jnp.zeros_like(kprev_ref)

    def rope(x):
        return x * cos + pltpu.roll(x, HEAD_DIM // 2, axis=1) * sin

    row = lax.broadcasted_iota(jnp.int32, (blk, 2 * blk), 0)
    col = lax.broadcasted_iota(jnp.int32, (blk, 2 * blk), 1)
    dist = blk + row - col
    kpos = (n - 1) * blk + col
    valid = (dist >= 0) & (dist < blk) & (kpos >= 0)
    scale = HEAD_DIM ** -0.5

    scores, values = [], []
    for kv in range(SWA_KV_HEADS):
        ks = slice(kv * HEAD_DIM, (kv + 1) * HEAD_DIM)
        k_cur = rope(kc_ref[:, ks]).astype(BF16)
        k = jnp.concatenate([kprev_ref[kv], k_cur], axis=0)
        kprev_ref[kv] = k_cur
        values.append(jnp.concatenate([vp_ref[:, ks], vc_ref[:, ks]], axis=0).astype(BF16))
        for g in range(SWA_GROUP):
            h = kv * SWA_GROUP + g
            q = rope(q_ref[:, h * HEAD_DIM:(h + 1) * HEAD_DIM]).astype(BF16)
            scores.append(lax.dot_general(q, k, (((1,), (1,)), ((), ())),
                                          preferred_element_type=F32))
    probs, denoms = [], []
    for h, s in enumerate(scores):
        s = jnp.where(valid, s * scale, NEG_INF)
        sink = sink_ref[h]
        m = jnp.maximum(jnp.max(s, axis=-1, keepdims=True), sink)
        p = jnp.exp(s - m)
        denoms.append(jnp.sum(p[:, :blk] + p[:, blk:], axis=-1, keepdims=True) + jnp.exp(sink - m))
        probs.append(p.astype(BF16))
    outs = []
    sq = jnp.zeros((blk, HEAD_DIM), F32)
    for h, (p, denom) in enumerate(zip(probs, denoms)):
        o = jnp.dot(p, values[h // SWA_GROUP], preferred_element_type=F32) / denom
        outs.append(o)
        sq = sq + o * o
    inv = lax.rsqrt(jnp.sum(sq, axis=-1, keepdims=True) / SWA_OUT + NORM_EPS)
    for h, o in enumerate(outs):
        hs = slice(h * HEAD_DIM, (h + 1) * HEAD_DIM)
        o_ref[:, hs] = (o * inv * gn_ref[:, hs]).astype(o_ref.dtype)


def _swa(z3, cos, sin, sinks, gn):
    b, s, _ = z3.shape
    blk = SWA_BLOCK
    nb = s // blk
    kcol = SWA_OUT // SWA_KV_OUT
    vcol = kcol + 1
    cur = lambda bi, n: (bi, n, 0)
    tab = pl.BlockSpec((None, blk, LANE), cur)
    return pl.pallas_call(
        _swa_kernel,
        grid=(b, nb),
        in_specs=[
            pl.BlockSpec(memory_space=pltpu.SMEM),
            pl.BlockSpec((None, blk, SWA_OUT), cur),
            pl.BlockSpec((None, blk, SWA_KV_OUT), lambda bi, n: (bi, n, kcol)),
            pl.BlockSpec((None, blk, SWA_KV_OUT), lambda bi, n: (bi, n, vcol)),
            pl.BlockSpec((None, blk, SWA_KV_OUT), lambda bi, n: (bi, jnp.maximum(n - 1, 0), vcol)),
            tab, tab,
            pl.BlockSpec((1, SWA_OUT), lambda bi, n: (0, 0)),
        ],
        out_specs=pl.BlockSpec((None, blk, SWA_OUT), cur),
        out_shape=jax.ShapeDtypeStruct((b, s, SWA_OUT), BF16),
        scratch_shapes=[pltpu.VMEM((SWA_KV_HEADS, blk, HEAD_DIM), BF16)],
        compiler_params=_params("arbitrary", "arbitrary"),
        name="swa",
    )(sinks, z3, z3, z3, z3, cos, sin, gn[None, :])


RG_CHUNK = 256


def _rglru_kernel(x0_ref, x1_ref, g0_ref, g1_ref, cw_ref, cb_ref, wa_ref, ba_ref, wx_ref, bx_ref,
                  lam_ref, gn_ref, o_ref, xe_ref, a_ref, b_ref, h_ref, carry_ref):
    lc = RG_CHUNK
    pad = SUBLANE

    @pl.when(pl.program_id(1) == 0)
    def _():
        xe_ref[0:pad, :] = jnp.zeros((pad, RG_WIDTH), F32)
        carry_ref[...] = jnp.zeros_like(carry_ref)

    xe_ref[pad:pad + lc, :] = jnp.concatenate([x0_ref[...], x1_ref[...]], axis=1)
    cw = cw_ref[...]
    xc = cb_ref[...]
    for j in range(RG_CONV):
        sh = RG_CONV - 1 - j
        xc = xc + xe_ref[pad - sh:pad - sh + lc, :] * cw[j:j + 1, :]
    xe_ref[0:pad, :] = xe_ref[lc:lc + pad, :]

    xcb = xc.astype(BF16)

    def gate(w_ref, bias_ref):
        parts = [jnp.dot(xcb[:, nb * RG_BLOCK_DIM:(nb + 1) * RG_BLOCK_DIM], w_ref[nb],
                         preferred_element_type=F32) for nb in range(RG_BLOCKS)]
        return jax.nn.sigmoid(jnp.concatenate(parts, axis=1) + bias_ref[...])

    r = gate(wa_ref, ba_ref)
    i = gate(wx_ref, bx_ref)
    nlam = -lam_ref[...]
    softplus = jnp.maximum(nlam, 0.0) + jnp.log1p(jnp.exp(-jnp.abs(nlam)))
    log_a = (-RG_C * softplus) * r
    a = jnp.exp(log_a)
    one_minus_a2 = -jnp.tanh(log_a) * (a * a + 1.0)
    a_ref[...] = a
    b_ref[...] = jnp.sqrt(one_minus_a2) * i * xc

    sub = lax.broadcasted_iota(jnp.int32, (SUBLANE, RG_WIDTH), 0)

    def scan_tile(t, h):
        r0 = pl.multiple_of(t * SUBLANE, SUBLANE)
        at = a_ref[pl.ds(r0, SUBLANE), :]
        bt = b_ref[pl.ds(r0, SUBLANE), :]
        for sh in (1, 2, 4):
            keep = sub >= sh
            a_sh = pltpu.roll(at, sh, axis=0)
            b_sh = pltpu.roll(bt, sh, axis=0)
            bt = jnp.where(keep, at * b_sh + bt, bt)
            at = jnp.where(keep, at * a_sh, at)
        ht = at * h + bt
        h_ref[pl.ds(r0, SUBLANE), :] = ht
        return ht[SUBLANE - 1:SUBLANE, :]

    carry_ref[0:1, :] = lax.fori_loop(0, lc // SUBLANE, scan_tile, carry_ref[0:1, :], unroll=4)

    g = jnp.concatenate([g0_ref[...], g1_ref[...]], axis=1)
    y = h_ref[...] * jax.nn.gelu(g)
    o_ref[...] = _rms(y, gn_ref[...]).astype(o_ref.dtype)


def _rglru(z3, conv_w, conv_b, wa, ba, wx, bx, lam, gn):
    b, s, _ = z3.shape
    lc = RG_CHUNK
    half = RG_WIDTH // 2
    x_col = (SWA_OUT + 2 * SWA_KV_OUT) // half
    g_col = x_col + 2

    def zcol(c):
        return pl.BlockSpec((None, lc, half), lambda bi, ci: (bi, ci, c))

    row = pl.BlockSpec((1, RG_WIDTH), lambda bi, ci: (0, 0))
    wspec = pl.BlockSpec((RG_BLOCKS, RG_BLOCK_DIM, RG_BLOCK_DIM), lambda bi, ci: (0, 0, 0))
    return pl.pallas_call(
        _rglru_kernel,
        grid=(b, s // lc),
        in_specs=[zcol(x_col), zcol(x_col + 1), zcol(g_col), zcol(g_col + 1),
                  pl.BlockSpec((RG_CONV, RG_WIDTH), lambda bi, ci: (0, 0)), row,
                  wspec, row, wspec, row, row, row],
        out_specs=pl.BlockSpec((None, lc, RG_WIDTH), lambda bi, ci: (bi, ci, 0)),
        out_shape=jax.ShapeDtypeStruct((b, s, RG_WIDTH), BF16),
        scratch_shapes=[pltpu.VMEM((lc + SUBLANE, RG_WIDTH), F32),
                        pltpu.VMEM((lc, RG_WIDTH), F32),
                        pltpu.VMEM((lc, RG_WIDTH), F32),
                        pltpu.VMEM((lc, RG_WIDTH), F32),
                        pltpu.VMEM((SUBLANE, RG_WIDTH), F32)],
        compiler_params=_params("parallel", "arbitrary"),
        name="rglru",
    )(z3, z3, z3, z3, conv_w, conv_b[None, :], wa, ba[None, :], wx, bx[None, :],
      lam[None, :], gn[None, :])


def _rope64(x, cos, sin, first_half):
    swapped = jnp.where(first_half, pltpu.roll(x, LANE - MLA_ROPE // 2, axis=1),
                        pltpu.roll(x, MLA_ROPE // 2, axis=1))
    return x * cos + swapped * sin


def _mla_proj_kernel(cq0_ref, cq1_ref, ckv_ref, kr_ref, cos_ref, sin_ref, qn_ref, kvn_ref,
                     wqn_ref, wqr_ref, wkn_ref, wv_ref, q_out, kn_out, kr_out, v_out):
    cos, sin = cos_ref[...], sin_ref[...]
    tm = cos.shape[0]
    lane = lax.broadcasted_iota(jnp.int32, (tm, LANE), 1)
    first_half = (lane % MLA_ROPE) < (MLA_ROPE // 2)
    low = lane < MLA_ROPE

    hq = _rms(jnp.concatenate([cq0_ref[...], cq1_ref[...]], axis=1), qn_ref[...]).astype(BF16)
    qn = jnp.dot(hq, wqn_ref[...], preferred_element_type=F32) * MLA_Q_SCALE
    qr = jnp.dot(hq, wqr_ref[...], preferred_element_type=F32) * MLA_Q_SCALE
    for pair in range(MLA_HEADS // 2):
        rot = _rope64(qr[:, pair * LANE:(pair + 1) * LANE], cos, sin, first_half)
        for odd in range(2):
            h = 2 * pair + odd
            piece = pltpu.roll(rot, MLA_ROPE, axis=1) if odd else rot
            q_out[h, :, 0:MLA_NOPE] = qn[:, h * MLA_NOPE:(h + 1) * MLA_NOPE].astype(BF16)
            q_out[h, :, MLA_NOPE:MLA_QK_PAD] = jnp.where(low, piece, 0.0).astype(BF16)

    hkv = _rms(ckv_ref[...], kvn_ref[...]).astype(BF16)
    kn = jnp.dot(hkv, wkn_ref[...], preferred_element_type=F32)
    v = jnp.dot(hkv, wv_ref[...], preferred_element_type=F32)
    for h in range(MLA_HEADS):
        hs = slice(h * LANE, (h + 1) * LANE)
        kn_out[h] = kn[:, hs].astype(BF16)
        v_out[h] = v[:, hs].astype(BF16)
    kr_out[...] = _rope64(kr_ref[...], cos, sin, first_half).astype(BF16)


def _mla_proj(z, kr, cos64, sin64, q_norm, kv_norm, wqn, wqr, wkn, wv, b, s):
    t = z.shape[0]
    tm = 512
    per_b = s // tm
    half = MLA_Q_RANK // 2
    cq_col = (SWA_OUT + 2 * SWA_KV_OUT + 2 * RG_WIDTH) // half
    tok = lambda i: (i, 0)
    head_blk = lambda i: (i // per_b, 0, i % per_b, 0)
    full = lambda shape: pl.BlockSpec(shape, lambda i: (0, 0))
    return pl.pallas_call(
        _mla_proj_kernel,
        grid=(t // tm,),
        in_specs=[pl.BlockSpec((tm, half), lambda i: (i, cq_col)),
                  pl.BlockSpec((tm, half), lambda i: (i, cq_col + 1)),
                  pl.BlockSpec((tm, MLA_KV_RANK), lambda i: (i, cq_col + 2)),
                  pl.BlockSpec((tm, LANE), tok), pl.BlockSpec((tm, LANE), tok),
                  pl.BlockSpec((tm, LANE), tok),
                  full((1, MLA_Q_RANK)), full((1, MLA_KV_RANK)),
                  full(wqn.shape), full(wqr.shape), full(wkn.shape), full(wv.shape)],
        out_specs=[pl.BlockSpec((None, MLA_HEADS, tm, MLA_QK_PAD), head_blk),
                   pl.BlockSpec((None, MLA_HEADS, tm, LANE), head_blk),
                   pl.BlockSpec((None, tm, LANE), lambda i: (i // per_b, i % per_b, 0)),
                   pl.BlockSpec((None, MLA_HEADS, tm, LANE), head_blk)],
        out_shape=[jax.ShapeDtypeStruct((b, MLA_HEADS, s, MLA_QK_PAD), BF16),
                   jax.ShapeDtypeStruct((b, MLA_HEADS, s, LANE), BF16),
                   jax.ShapeDtypeStruct((b, s, LANE), BF16),
                   jax.ShapeDtypeStruct((b, MLA_HEADS, s, LANE), BF16)],
        compiler_params=_params("parallel"),
        name="mla_proj",
    )(z, z, z, kr, cos64, sin64, q_norm[None, :], kv_norm[None, :], wqn, wqr, wkn, wv)


MLA_TQ = 512
MLA_HEAD_GROUP = 4


def _mla_attn_kernel(qi_tab, kj_tab, q_ref, kn_ref, kr_ref, v_ref, gn_ref, o_ref,
                     m_ref, l_ref, acc_ref):
    tq = MLA_TQ
    step = pl.program_id(1)
    qi = qi_tab[step]
    kj = kj_tab[step]

    @pl.when(kj == 0)
    def _():
        m_ref[...] = jnp.full(m_ref.shape, NEG_INF, F32)
        l_ref[...] = jnp.zeros_like(l_ref)
        acc_ref[...] = jnp.zeros_like(acc_ref)

    kr = kr_ref[...]

    def attend(masked):
        if masked:
            row = lax.broadcasted_iota(jnp.int32, (tq, tq), 0)
            col = lax.broadcasted_iota(jnp.int32, (tq, tq), 1)
            causal = col <= row
        for h0 in range(0, MLA_HEADS, MLA_HEAD_GROUP):
            heads = range(h0, h0 + MLA_HEAD_GROUP)
            scores = []
            for h in heads:
                k = jnp.concatenate([kn_ref[h], kr], axis=1)
                scores.append(lax.dot_general(q_ref[h], k, (((1,), (1,)), ((), ())),
                                              preferred_element_type=F32))
            probs, alphas = [], []
            for h, s in zip(heads, scores):
                if masked:
                    s = jnp.where(causal, s, NEG_INF)
                m_old = m_ref[h]
                m_new = jnp.maximum(m_old, jnp.max(s, axis=-1, keepdims=True))
                alpha = jnp.exp2(m_old - m_new)
                p = jnp.exp2(s - jnp.concatenate([m_new] * (tq // LANE), axis=1))
                l_ref[h] = alpha * l_ref[h] + sum(p[:, c:c + LANE] for c in range(0, tq, LANE))
                m_ref[h] = m_new
                probs.append(p.astype(BF16))
                alphas.append(alpha)
            for h, p, alpha in zip(heads, probs, alphas):
                acc_ref[h] = alpha * acc_ref[h] + jnp.dot(p, v_ref[h], preferred_element_type=F32)

    @pl.when(kj != qi)
    def _():
        attend(False)

    @pl.when(kj == qi)
    def _():
        attend(True)

    @pl.when(kj == qi)
    def _():
        sq = jnp.zeros((tq, MLA_V), F32)
        for h in range(MLA_HEADS):
            o = acc_ref[h] / jnp.sum(l_ref[h], axis=-1, keepdims=True)
            acc_ref[h] = o
            sq = sq + o * o
        inv = lax.rsqrt(jnp.sum(sq, axis=-1, keepdims=True) / MLA_OUT + NORM_EPS)
        for h in range(MLA_HEADS):
            hs = slice(h * MLA_V, (h + 1) * MLA_V)
            o_ref[:, hs] = (acc_ref[h] * inv * gn_ref[:, hs]).astype(o_ref.dtype)


def _mla_attn(q, kn, kr, v, gn):
    b, _, s, _ = q.shape
    tq = MLA_TQ
    nq = s // tq
    pairs = [(qi, kj) for qi in range(nq) for kj in range(qi + 1)]
    qi_tab = jnp.asarray(np.array([pr[0] for pr in pairs], np.int32))
    kj_tab = jnp.asarray(np.array([pr[1] for pr in pairs], np.int32))
    head_rows = lambda tab_of: (lambda bi, st, qt, kt: (bi, 0, tab_of(qt, kt)[st], 0))
    by_q = lambda qt, kt: qt
    by_k = lambda qt, kt: kt
    grid_spec = pltpu.PrefetchScalarGridSpec(
        num_scalar_prefetch=2,
        grid=(b, len(pairs)),
        in_specs=[pl.BlockSpec((None, MLA_HEADS, tq, MLA_QK_PAD), head_rows(by_q)),
                  pl.BlockSpec((None, MLA_HEADS, tq, LANE), head_rows(by_k)),
                  pl.BlockSpec((None, tq, LANE), lambda bi, st, qt, kt: (bi, kt[st], 0)),
                  pl.BlockSpec((None, MLA_HEADS, tq, LANE), head_rows(by_k)),
                  pl.BlockSpec((1, MLA_OUT), lambda bi, st, qt, kt: (0, 0))],
        out_specs=pl.BlockSpec((None, tq, MLA_OUT), lambda bi, st, qt, kt: (bi, qt[st], 0)),
        scratch_shapes=[pltpu.VMEM((MLA_HEADS, tq, LANE), F32),
                        pltpu.VMEM((MLA_HEADS, tq, LANE), F32),
                        pltpu.VMEM((MLA_HEADS, tq, MLA_V), F32)],
    )
    return pl.pallas_call(
        _mla_attn_kernel,
        grid_spec=grid_spec,
        out_shape=jax.ShapeDtypeStruct((b, s, MLA_OUT), BF16),
        compiler_params=_params("parallel", "arbitrary"),
        name="mla_attn",
    )(qi_tab, kj_tab, q, kn, kr, v, gn[None, :])


def kernel(x, p, positions, pre_mix_norm, w_in, swa_sinks, rg_conv_w, rg_conv_b, rg_gate_a_w,
           rg_gate_a_b, rg_gate_x_w, rg_gate_x_b, rg_lambda, mla_q_norm, mla_w_uq, mla_kv_norm,
           mla_w_ukv, group_norm, w_out, post_mix_norm, pre_ffn_norm, w_gate, w_up, w_down,
           post_ffn_norm, w_ple, ple_norm, w_ple_gate, b_ple_gate):
    b, s, d = x.shape
    t = b * s
    depth = w_in.shape[0]

    pos_col = positions.astype(F32).reshape(t, 1)
    cos128, sin128 = _rope_tables(pos_col, HEAD_DIM)
    cos64, sin64 = _rope_tables(pos_col, MLA_ROPE)
    cos128_3, sin128_3 = cos128.reshape(b, s, LANE), sin128.reshape(b, s, LANE)

    w_in_t = jnp.swapaxes(w_in, 1, 2)
    xf = x.reshape(t, d)
    h = _norm_cast(xf, pre_mix_norm[0])
    for i in range(depth):
        z = _matmul_wstream([h], [w_in_t], i, 1024, 1024, n_cols=Z_MAIN, w_transposed=True,
                            name="in_proj")
        w_kr = jnp.pad(w_in_t[i, Z_MAIN:, :], ((0, LANE - MLA_ROPE), (0, 0)))[None]
        kr = _matmul([h], w_kr, 0, 1024, LANE, w_transposed=True, name="in_proj_kr")
        z3 = z.reshape(b, s, Z_MAIN)
        gn = group_norm[i]

        o_a = _swa(z3, cos128_3, sin128_3, swa_sinks[i], gn[:SWA_OUT])
        o_b = _rglru(z3, rg_conv_w[i], rg_conv_b[i], rg_gate_a_w[i].astype(BF16), rg_gate_a_b[i],
                     rg_gate_x_w[i].astype(BF16), rg_gate_x_b[i], rg_lambda[i],
                     gn[SWA_OUT:SWA_OUT + RG_WIDTH])
        w_uq = mla_w_uq[i].astype(BF16).reshape(MLA_Q_RANK, MLA_HEADS, MLA_NOPE + MLA_ROPE)
        w_ukv = mla_w_ukv[i].astype(BF16).reshape(MLA_KV_RANK, MLA_HEADS, MLA_NOPE + MLA_V)
        q_c, kn_c, kr_c, v_c = _mla_proj(
            z, kr, cos64, sin64, mla_q_norm[i], mla_kv_norm[i],
            w_uq[:, :, :MLA_NOPE].reshape(MLA_Q_RANK, MLA_HEADS * MLA_NOPE),
            w_uq[:, :, MLA_NOPE:].reshape(MLA_Q_RANK, MLA_HEADS * MLA_ROPE),
            w_ukv[:, :, :MLA_NOPE].reshape(MLA_KV_RANK, MLA_HEADS * MLA_NOPE),
            w_ukv[:, :, MLA_NOPE:].reshape(MLA_KV_RANK, MLA_HEADS * MLA_V), b, s)
        o_c = _mla_attn(q_c, kn_c, kr_c, v_c, gn[SWA_OUT + RG_WIDTH:])

        mix = _matmul_wstream(
            [o_a.reshape(t, SWA_OUT), o_b.reshape(t, RG_WIDTH), o_c.reshape(t, MLA_OUT)],
            [w_out], i, 1024, 1024, name="out_proj")
        xf, h = _resid_norm(xf, mix[None], post_mix_norm[i], pre_ffn_norm[i])

        gu = _matmul_wstream([h], [w_gate, w_up], i, 1024, 512, out_dtype=BF16, swiglu=True,
                             name="ffn_up")
        f = _matmul_wstream([gu], [w_down], i, 512, 512, k_sizes=[w_down.shape[1]], name="ffn_down")
        xf, xb = _resid_norm(xf, f[None], post_ffn_norm[i], None)

        gate = _matmul_wstream([xb], [w_ple_gate], i, 1024, 1024, name="ple_gate")
        g_next = pre_mix_norm[i + 1] if i + 1 < depth else None
        xf, h = _ple(xf, gate, p[i].reshape(t, PLE_DIM), w_ple[i].astype(BF16), ple_norm[i],
                     b_ple_gate[i], g_next)
    return xf.reshape(b, s, d)
```

```python
import functools
import math

import jax
import jax.numpy as jnp
import numpy as np
from jax import lax
from jax.experimental import pallas as pl
from jax.experimental.pallas import tpu as pltpu

F32 = jnp.float32
BF16 = jnp.bfloat16

D_MODEL = 4096
HEAD_DIM = 128
ROPE_THETA = 10000.0
NORM_EPS = 1e-6
PLE_DIM = 256
NEG_INF = -1e30

SWA_HEADS = 12
SWA_KV_HEADS = 4
SWA_GROUP = SWA_HEADS // SWA_KV_HEADS
SWA_BLOCK = 128
SWA_OUT = SWA_HEADS * HEAD_DIM
SWA_KV_OUT = SWA_KV_HEADS * HEAD_DIM

RG_WIDTH = 1024
RG_BLOCKS = 8
RG_BLOCK_DIM = RG_WIDTH // RG_BLOCKS
RG_CONV = 4
RG_C = 8.0

MLA_HEADS = 12
MLA_Q_RANK = 1024
MLA_KV_RANK = 512
MLA_NOPE = 128
MLA_ROPE = 64
MLA_V = 128
MLA_OUT = MLA_HEADS * MLA_V
MLA_QK_PAD = 256
MLA_Q_SCALE = (MLA_NOPE + MLA_ROPE) ** -0.5 * math.log2(math.e)

MIX_WIDTH = SWA_OUT + RG_WIDTH + MLA_OUT
Z_MAIN = SWA_OUT + 2 * SWA_KV_OUT + 2 * RG_WIDTH + MLA_Q_RANK + MLA_KV_RANK

V7X_VMEM_LIMIT_BYTES = 56 * 1024 * 1024
LANE = 128
SUBLANE = 8


def _params(*semantics):
    return pltpu.CompilerParams(dimension_semantics=semantics,
                                vmem_limit_bytes=V7X_VMEM_LIMIT_BYTES)


def _rms(xf, g):
    ms = jnp.mean(xf * xf, axis=-1, keepdims=True)
    return xf * lax.rsqrt(ms + NORM_EPS) * g


def _rope_table_kernel(pos_ref, inv_ref, sign_ref, cos_ref, sin_ref):
    ang = pos_ref[...] * inv_ref[...]
    cos_ref[...] = jnp.cos(ang)
    sin_ref[...] = jnp.sin(ang) * sign_ref[...]


def _rope_tables(pos_col, rot_dim):
    t = pos_col.shape[0]
    half = rot_dim // 2
    inv = ROPE_THETA ** (-jnp.arange(0, rot_dim, 2, dtype=F32) / rot_dim)
    inv_row = jnp.tile(inv, LANE // half)[None, :]
    sign_row = jnp.tile(jnp.concatenate([-jnp.ones((half,), F32), jnp.ones((half,), F32)]),
                        LANE // rot_dim)[None, :]
    tm = 1024
    row = pl.BlockSpec((1, LANE), lambda i: (0, 0))
    return pl.pallas_call(
        _rope_table_kernel,
        grid=(t // tm,),
        in_specs=[pl.BlockSpec((tm, 1), lambda i: (i, 0)), row, row],
        out_specs=[pl.BlockSpec((tm, LANE), lambda i: (i, 0))] * 2,
        out_shape=[jax.ShapeDtypeStruct((t, LANE), F32)] * 2,
        compiler_params=_params("parallel"),
        name="rope_tables",
    )(pos_col, inv_row, sign_row)


def _norm_cast_kernel(x_ref, g_ref, o_ref):
    o_ref[...] = _rms(x_ref[...], g_ref[...]).astype(o_ref.dtype)


def _norm_cast(x, g):
    t, d = x.shape
    tm = 256
    return pl.pallas_call(
        _norm_cast_kernel,
        grid=(t // tm,),
        in_specs=[pl.BlockSpec((tm, d), lambda i: (i, 0)),
                  pl.BlockSpec((1, d), lambda i: (0, 0))],
        out_specs=pl.BlockSpec((tm, d), lambda i: (i, 0)),
        out_shape=jax.ShapeDtypeStruct((t, d), BF16),
        compiler_params=_params("parallel"),
        name="norm_cast",
    )(x, g[None, :])


def _mm_kernel(*refs, k_sizes, w_transposed):
    n_in = len(k_sizes)
    w_ref, o_ref, wb_ref = refs[n_in], refs[n_in + 1], refs[n_in + 2]

    @pl.when(pl.program_id(2) == 0)
    def _():
        wb_ref[...] = w_ref[...].astype(BF16)

    acc = None
    off = 0
    for x_ref, ks in zip(refs[:n_in], k_sizes):
        if w_transposed:
            part = lax.dot_general(x_ref[...], wb_ref[:, off:off + ks], (((1,), (1,)), ((), ())),
                                   preferred_element_type=F32)
        else:
            part = jnp.dot(x_ref[...], wb_ref[off:off + ks, :], preferred_element_type=F32)
        acc = part if acc is None else acc + part
        off += ks
    o_ref[...] = acc.astype(o_ref.dtype)


def _matmul(xs, w_stack, layer, tm, tn, n_cols=None, k_chunks=1, out_dtype=F32,
            w_transposed=False, name="matmul"):
    m = xs[0].shape[0]
    if w_transposed:
        _, n, k = w_stack.shape
    else:
        _, k, n = w_stack.shape
    n_cols = n if n_cols is None else n_cols
    k_sizes = tuple(x.shape[1] for x in xs)
    assert m % tm == 0 and n_cols % tn == 0 and k % k_chunks == 0
    tk = k // k_chunks
    if k_chunks == 1:
        assert sum(k_sizes) == k
        in_specs = [pl.BlockSpec((tm, ks), lambda j, c, i: (i, 0)) for ks in k_sizes]
        out_specs = pl.BlockSpec((tm, tn), lambda j, c, i: (i, j))
        out_shape = jax.ShapeDtypeStruct((m, n_cols), out_dtype)
    else:
        assert k_sizes == (k,)
        k_sizes = (tk,)
        in_specs = [pl.BlockSpec((tm, tk), lambda j, c, i: (i, c))]
        out_specs = pl.BlockSpec((None, tm, tn), lambda j, c, i: (c, i, j))
        out_shape = jax.ShapeDtypeStruct((k_chunks, m, n_cols), out_dtype)
    if w_transposed:
        w_block = (tn, tk)
        in_specs.append(pl.BlockSpec((None, tn, tk), lambda j, c, i: (layer, j, c)))
    else:
        w_block = (tk, tn)
        in_specs.append(pl.BlockSpec((None, tk, tn), lambda j, c, i: (layer, c, j)))
    return pl.pallas_call(
        functools.partial(_mm_kernel, k_sizes=k_sizes, w_transposed=w_transposed),
        grid=(n_cols // tn, k_chunks, m // tm),
        in_specs=in_specs,
        out_specs=out_specs,
        out_shape=out_shape,
        scratch_shapes=[pltpu.VMEM(w_block, BF16)],
        compiler_params=_params("arbitrary", "arbitrary", "arbitrary"),
        name=name,
    )(*xs, w_stack)


def _ffn_up_kernel(x_ref, wg_ref, wu_ref, o_ref, wgb_ref, wub_ref):
    @pl.when(pl.program_id(1) == 0)
    def _():
        wgb_ref[...] = wg_ref[...].astype(BF16)
        wub_ref[...] = wu_ref[...].astype(BF16)

    x = x_ref[...]
    g = jnp.dot(x, wgb_ref[...], preferred_element_type=F32)
    u = jnp.dot(x, wub_ref[...], preferred_element_type=F32)
    o_ref[...] = (g * jax.nn.sigmoid(g) * u).astype(o_ref.dtype)


def _ffn_up(h, wg_stack, wu_stack, layer, tm, tn):
    m, k = h.shape
    n = wg_stack.shape[-1]
    assert m % tm == 0 and n % tn == 0
    wspec = pl.BlockSpec((None, k, tn), lambda j, i: (layer, 0, j))
    return pl.pallas_call(
        _ffn_up_kernel,
        grid=(n // tn, m // tm),
        in_specs=[pl.BlockSpec((tm, k), lambda j, i: (i, 0)), wspec, wspec],
        out_specs=pl.BlockSpec((tm, tn), lambda j, i: (i, j)),
        out_shape=jax.ShapeDtypeStruct((m, n), BF16),
        scratch_shapes=[pltpu.VMEM((k, tn), BF16), pltpu.VMEM((k, tn), BF16)],
        compiler_params=_params("arbitrary", "arbitrary"),
        name="ffn_up",
    )(h, wg_stack, wu_stack)


def _wstream_kernel(*refs, k_sizes, n_w, layer, w_transposed, tn, n_total, rc, n_chunks, n_blocks,
                    swiglu):
    n_x = len(k_sizes)
    x_refs = refs[:n_x]
    w_hbm = refs[n_x:n_x + n_w]
    o_ref = refs[n_x + n_w]
    wb = refs[n_x + n_w + 1:n_x + 2 * n_w + 1]
    st = refs[n_x + 2 * n_w + 1:n_x + 3 * n_w + 1]
    sem = refs[n_x + 3 * n_w + 1]
    j, i = pl.program_id(0), pl.program_id(1)
    q = j * n_chunks + i
    slot = j % 2
    shift = (tn - n_total % tn) % tn

    def col0(blk):
        return pl.multiple_of(jnp.minimum(blk * tn, n_total - tn), LANE)

    def chunk_copy(w, blk, ch, buf):
        r0 = pl.multiple_of(ch * rc, SUBLANE)
        if w_transposed:
            src = w_hbm[w].at[layer, pl.ds(pl.multiple_of(col0(blk) + r0, SUBLANE), rc), :]
        else:
            src = w_hbm[w].at[layer, pl.ds(r0, rc), pl.ds(col0(blk), tn)]
        return pltpu.make_async_copy(src, st[w].at[buf], sem.at[w, buf])

    def cast_chunk(w, ch, buf, dst_slot):
        r0 = pl.multiple_of(ch * rc, 2 * SUBLANE)
        wb[w][dst_slot, pl.ds(r0, rc), :] = st[w][buf].astype(BF16)

    @pl.when(q == 0)
    def _():
        for w in range(n_w):
            chunk_copy(w, 0, 0, 0).start()
        for ch in range(n_chunks):
            buf = ch % 2
            for w in range(n_w):
                if ch + 1 < n_chunks:
                    chunk_copy(w, 0, ch + 1, 1 - buf).start()
                chunk_copy(w, 0, ch, buf).wait()
                cast_chunk(w, ch, buf, 0)
        for w in range(n_w):
            chunk_copy(w, 0, n_chunks - 1, 1).start()

    nxt = (j + 1) % n_blocks
    pbuf = (q + 1) % 2
    pch = (i + n_chunks - 1) % n_chunks
    pblk = jnp.where(i >= 1, nxt, j)
    pslot = jnp.where(i >= 1, 1 - slot, slot)
    for w in range(n_w):
        chunk_copy(w, pblk, pch, pbuf).wait()
    for w in range(n_w):
        chunk_copy(w, nxt, i, q % 2).start()
    def product(w, c0):
        acc = None
        off = 0
        for x_ref, ks in zip(x_refs, k_sizes):
            if w_transposed:
                part = lax.dot_general(x_ref[...], wb[w][slot, c0:, off:off + ks],
                                       (((1,), (1,)), ((), ())), preferred_element_type=F32)
            else:
                part = jnp.dot(x_ref[...], wb[w][slot, off:off + ks, c0:],
                               preferred_element_type=F32)
            acc = part if acc is None else acc + part
            off += ks
        return acc

    def compute(c0):
        for w in range(n_w):
            cast_chunk(w, pch, pbuf, pslot)
        out = product(0, c0)
        if swiglu:
            out = out * jax.nn.sigmoid(out) * product(1, c0)
        o_ref[:, :tn - c0] = out.astype(o_ref.dtype)
        if c0:
            o_ref[:, tn - c0:] = jnp.zeros((o_ref.shape[0], c0), o_ref.dtype)

    if shift == 0:
        compute(0)
    else:
        @pl.when(j != n_blocks - 1)
        def _():
            compute(0)

        @pl.when(j == n_blocks - 1)
        def _():
            compute(shift)

    @pl.when(q == n_blocks * n_chunks - 1)
    def _():
        for w in range(n_w):
            chunk_copy(w, nxt, i, q % 2).wait()


def _matmul_wstream(xs, w_stacks, layer, tm, tn, n_cols=None, k_sizes=None, out_dtype=F32,
                    w_transposed=False, swiglu=False, name="matmul"):
    m = xs[0].shape[0]
    k_sizes = tuple(x.shape[1] for x in xs) if k_sizes is None else tuple(k_sizes)
    k = sum(k_sizes)
    if w_transposed:
        _, n, kw = w_stacks[0].shape
        rows, cols = tn, k
    else:
        _, kw, n = w_stacks[0].shape
        rows, cols = k, tn
    n_cols = n if n_cols is None else n_cols
    n_chunks = m // tm
    rc = rows // n_chunks
    n_blocks = -(-n_cols // tn)
    n_w = len(w_stacks)
    assert kw == k and m % tm == 0 and rows % n_chunks == 0 and rc % (2 * SUBLANE) == 0
    assert n_blocks >= 2 and n_cols >= tn and (not w_transposed or n_cols % tn == 0)
    assert n_w == (2 if swiglu else 1)
    in_specs = [pl.BlockSpec((tm, ks), lambda j, i: (i, 0)) for ks in k_sizes]
    in_specs += [pl.BlockSpec(memory_space=pl.ANY)] * n_w
    scratch = [pltpu.VMEM((2, rows, cols), BF16)] * n_w
    scratch += [pltpu.VMEM((2, rc, cols), F32)] * n_w
    scratch.append(pltpu.SemaphoreType.DMA((n_w, 2)))
    return pl.pallas_call(
        functools.partial(_wstream_kernel, k_sizes=k_sizes, n_w=n_w, layer=layer,
                          w_transposed=w_transposed, tn=tn, n_total=n_cols, rc=rc,
                          n_chunks=n_chunks, n_blocks=n_blocks, swiglu=swiglu),
        grid=(n_blocks, n_chunks),
        in_specs=in_specs,
        out_specs=pl.BlockSpec((tm, tn), lambda j, i: (i, j)),
        out_shape=jax.ShapeDtypeStruct((m, n_blocks * tn), out_dtype),
        scratch_shapes=scratch,
        compiler_params=_params("arbitrary", "arbitrary"),
        name=name,
    )(*xs, *w_stacks)


def _resid_norm_kernel(x_ref, y_ref, gp_ref, gn_ref, xo_ref, ho_ref, *, norm_next):
    y = y_ref[0]
    for c in range(1, y_ref.shape[0]):
        y = y + y_ref[c]
    xn = x_ref[...] + _rms(y, gp_ref[...])
    xo_ref[...] = xn
    if norm_next:
        ho_ref[...] = _rms(xn, gn_ref[...]).astype(ho_ref.dtype)
    else:
        ho_ref[...] = xn.astype(ho_ref.dtype)


def _resid_norm(x, y_parts, g_post, g_next):
    t, d = x.shape
    parts = y_parts.shape[0]
    tm = 256
    norm_next = g_next is not None
    if g_next is None:
        g_next = g_post
    big = pl.BlockSpec((tm, d), lambda i: (i, 0))
    row = pl.BlockSpec((1, d), lambda i: (0, 0))
    return pl.pallas_call(
        functools.partial(_resid_norm_kernel, norm_next=norm_next),
        grid=(t // tm,),
        in_specs=[big, pl.BlockSpec((parts, tm, d), lambda i: (0, i, 0)), row, row],
        out_specs=[big, big],
        out_shape=[jax.ShapeDtypeStruct((t, d), F32), jax.ShapeDtypeStruct((t, d), BF16)],
        compiler_params=_params("parallel"),
        name="resid_norm",
    )(x, y_parts, g_post[None, :], g_next[None, :])


def _ple_kernel(x_ref, gate_ref, p_ref, wp_ref, pn_ref, b_ref, gn_ref, xo_ref, *maybe_ho_ref):
    e = _rms(jnp.dot(p_ref[...].astype(BF16), wp_ref[...], preferred_element_type=F32), pn_ref[...])
    xn = x_ref[...] + jax.nn.sigmoid(gate_ref[...] + b_ref[...]) * e
    xo_ref[...] = xn
    for ho_ref in maybe_ho_ref:
        ho_ref[...] = _rms(xn, gn_ref[...]).astype(ho_ref.dtype)


def _ple(x, gate, p, w_ple, ple_norm, b_gate, g_next):
    t, d = x.shape
    tm = 256
    norm_next = g_next is not None
    if g_next is None:
        g_next = ple_norm
    big = pl.BlockSpec((tm, d), lambda i: (i, 0))
    row = pl.BlockSpec((1, d), lambda i: (0, 0))
    out_specs = [big, big] if norm_next else [big]
    out_shape = [jax.ShapeDtypeStruct((t, d), F32)]
    if norm_next:
        out_shape.append(jax.ShapeDtypeStruct((t, d), BF16))
    outs = pl.pallas_call(
        _ple_kernel,
        grid=(t // tm,),
        in_specs=[big, big, pl.BlockSpec((tm, PLE_DIM), lambda i: (i, 0)),
                  pl.BlockSpec((PLE_DIM, d), lambda i: (0, 0)), row, row, row],
        out_specs=out_specs,
        out_shape=out_shape,
        compiler_params=_params("parallel"),
        name="ple",
    )(x, gate, p, w_ple, ple_norm[None, :], b_gate[None, :], g_next[None, :])
    return (outs[0], outs[1]) if norm_next else (outs[0], None)


SWA_KV_PER_PASS = 2


def _swa_kernel(sink_ref, q_ref, kc_ref, vc_ref, vp_ref, cos_ref, sin_ref, gn_ref, o_ref,
                kprev_ref):
    blk = SWA_BLOCK
    n = pl.program_id(1)
    cos, sin = cos_ref[...], sin_ref[...]

    @pl.when(n == 0)
    def _():
        kprev_ref[...] = jnp.zeros_like(kprev_ref)

    def rope(x):
        return x * cos + pltpu.roll(x, HEAD_DIM // 2, axis=1) * sin

    row = lax.broadcasted_iota(jnp.int32, (blk, 2 * blk), 0)
    col = lax.broadcasted_iota(jnp.int32, (blk, 2 * blk), 1)
    dist = blk + row - col
    kpos = (n - 1) * blk + col
    valid = (dist >= 0) & (dist < blk) & (kpos >= 0)
    scale = HEAD_DIM ** -0.5

    outs = []
    sq = jnp.zeros((blk, HEAD_DIM), F32)
    for kv0 in range(0, SWA_KV_HEADS, SWA_KV_PER_PASS):
        scores, values = [], []
        for kv in range(kv0, kv0 + SWA_KV_PER_PASS):
            ks = slice(kv * HEAD_DIM, (kv + 1) * HEAD_DIM)
            k_cur = rope(kc_ref[:, ks]).astype(BF16)
            k = jnp.concatenate([kprev_ref[kv], k_cur], axis=0)
            kprev_ref[kv] = k_cur
            values.append(jnp.concatenate([vp_ref[:, ks], vc_ref[:, ks]], axis=0).astype(BF16))
            for g in range(SWA_GROUP):
                h = kv * SWA_GROUP + g
                q = rope(q_ref[:, h * HEAD_DIM:(h + 1) * HEAD_DIM]).astype(BF16)
                scores.append(lax.dot_general(q, k, (((1,), (1,)), ((), ())),
                                              preferred_element_type=F32))
        probs, denoms = [], []
        for i, s in enumerate(scores):
            s = jnp.where(valid, s * scale, NEG_INF)
            sink = sink_ref[kv0 * SWA_GROUP + i]
            m = jnp.maximum(jnp.max(s, axis=-1, keepdims=True), sink)
            p = jnp.exp(s - m)
            denoms.append(jnp.sum(p[:, :blk] + p[:, blk:], axis=-1, keepdims=True)
                          + jnp.exp(sink - m))
            probs.append(p.astype(BF16))
        for i, (p, denom) in enumerate(zip(probs, denoms)):
            o = jnp.dot(p, values[i // SWA_GROUP], preferred_element_type=F32) / denom
            outs.append(o)
            sq = sq + o * o
    inv = lax.rsqrt(jnp.sum(sq, axis=-1, keepdims=True) / SWA_OUT + NORM_EPS)
    for h, o in enumerate(outs):
        hs = slice(h * HEAD_DIM, (h + 1) * HEAD_DIM)
        o_ref[:, hs] = (o * inv * gn_ref[:, hs]).astype(o_ref.dtype)


def _swa(z3, cos, sin, sinks, gn):
    b, s, _ = z3.shape
    blk = SWA_BLOCK
    nb = s // blk
    kcol = SWA_OUT // SWA_KV_OUT
    vcol = kcol + 1
    cur = lambda bi, n: (bi, n, 0)
    tab = pl.BlockSpec((None, blk, LANE), cur)
    return pl.pallas_call(
        _swa_kernel,
        grid=(b, nb),
        in_specs=[
            pl.BlockSpec(memory_space=pltpu.SMEM),
            pl.BlockSpec((None, blk, SWA_OUT), cur),
            pl.BlockSpec((None, blk, SWA_KV_OUT), lambda bi, n: (bi, n, kcol)),
            pl.BlockSpec((None, blk, SWA_KV_OUT), lambda bi, n: (bi, n, vcol)),
            pl.BlockSpec((None, blk, SWA_KV_OUT), lambda bi, n: (bi, jnp.maximum(n - 1, 0), vcol)),
            tab, tab,
            pl.BlockSpec((1, SWA_OUT), lambda bi, n: (0, 0)),
        ],
        out_specs=pl.BlockSpec((None, blk, SWA_OUT), cur),
        out_shape=jax.ShapeDtypeStruct((b, s, SWA_OUT), BF16),
        scratch_shapes=[pltpu.VMEM((SWA_KV_HEADS, blk, HEAD_DIM), BF16)],
        compiler_params=_params("arbitrary", "arbitrary"),
        name="swa",
    )(sinks, z3, z3, z3, z3, cos, sin, gn[None, :])


RG_CHUNK = 256


def _rglru_kernel(x0_ref, x1_ref, g0_ref, g1_ref, cw_ref, cb_ref, wa_ref, ba_ref, wx_ref, bx_ref,
                  lam_ref, gn_ref, o_ref, xe_ref, a_ref, b_ref, h_ref, carry_ref):
    lc = RG_CHUNK
    pad = SUBLANE

    @pl.when(pl.program_id(1) == 0)
    def _():
        xe_ref[0:pad, :] = jnp.zeros((pad, RG_WIDTH), F32)
        carry_ref[...] = jnp.zeros_like(carry_ref)

    xe_ref[pad:pad + lc, :] = jnp.concatenate([x0_ref[...], x1_ref[...]], axis=1)
    cw = cw_ref[...]
    xc = cb_ref[...]
    for j in range(RG_CONV):
        sh = RG_CONV - 1 - j
        xc = xc + xe_ref[pad - sh:pad - sh + lc, :] * cw[j:j + 1, :]
    xe_ref[0:pad, :] = xe_ref[lc:lc + pad, :]

    xcb = xc.astype(BF16)

    def gate(w_ref, bias_ref):
        parts = [jnp.dot(xcb[:, nb * RG_BLOCK_DIM:(nb + 1) * RG_BLOCK_DIM], w_ref[nb],
                         preferred_element_type=F32) for nb in range(RG_BLOCKS)]
        return jax.nn.sigmoid(jnp.concatenate(parts, axis=1) + bias_ref[...])

    r = gate(wa_ref, ba_ref)
    i = gate(wx_ref, bx_ref)
    nlam = -lam_ref[...]
    softplus = jnp.maximum(nlam, 0.0) + jnp.log1p(jnp.exp(-jnp.abs(nlam)))
    log_a = (-RG_C * softplus) * r
    a = jnp.exp(log_a)
    one_minus_a2 = -jnp.tanh(log_a) * (a * a + 1.0)
    a_ref[...] = a
    b_ref[...] = jnp.sqrt(one_minus_a2) * i * xc

    sub = lax.broadcasted_iota(jnp.int32, (SUBLANE, RG_WIDTH), 0)

    def scan_tile(t, h):
        r0 = pl.multiple_of(t * SUBLANE, SUBLANE)
        at = a_ref[pl.ds(r0, SUBLANE), :]
        bt = b_ref[pl.ds(r0, SUBLANE), :]
        for sh in (1, 2, 4):
            keep = sub >= sh
            a_sh = pltpu.roll(at, sh, axis=0)
            b_sh = pltpu.roll(bt, sh, axis=0)
            bt = jnp.where(keep, at * b_sh + bt, bt)
            at = jnp.where(keep, at * a_sh, at)
        ht = at * h + bt
        h_ref[pl.ds(r0, SUBLANE), :] = ht
        return ht[SUBLANE - 1:SUBLANE, :]

    carry_ref[0:1, :] = lax.fori_loop(0, lc // SUBLANE, scan_tile, carry_ref[0:1, :], unroll=4)

    g = jnp.concatenate([g0_ref[...], g1_ref[...]], axis=1)
    y = h_ref[...] * jax.nn.gelu(g)
    o_ref[...] = _rms(y, gn_ref[...]).astype(o_ref.dtype)


def _rglru(z3, conv_w, conv_b, wa, ba, wx, bx, lam, gn):
    b, s, _ = z3.shape
    lc = RG_CHUNK
    half = RG_WIDTH // 2
    x_col = (SWA_OUT + 2 * SWA_KV_OUT) // half
    g_col = x_col + 2

    def zcol(c):
        return pl.BlockSpec((None, lc, half), lambda bi, ci: (bi, ci, c))

    row = pl.BlockSpec((1, RG_WIDTH), lambda bi, ci: (0, 0))
    wspec = pl.BlockSpec((RG_BLOCKS, RG_BLOCK_DIM, RG_BLOCK_DIM), lambda bi, ci: (0, 0, 0))
    return pl.pallas_call(
        _rglru_kernel,
        grid=(b, s // lc),
        in_specs=[zcol(x_col), zcol(x_col + 1), zcol(g_col), zcol(g_col + 1),
                  pl.BlockSpec((RG_CONV, RG_WIDTH), lambda bi, ci: (0, 0)), row,
                  wspec, row, wspec, row, row, row],
        out_specs=pl.BlockSpec((None, lc, RG_WIDTH), lambda bi, ci: (bi, ci, 0)),
        out_shape=jax.ShapeDtypeStruct((b, s, RG_WIDTH), BF16),
        scratch_shapes=[pltpu.VMEM((lc + SUBLANE, RG_WIDTH), F32),
                        pltpu.VMEM((lc, RG_WIDTH), F32),
                        pltpu.VMEM((lc, RG_WIDTH), F32),
                        pltpu.VMEM((lc, RG_WIDTH), F32),
                        pltpu.VMEM((SUBLANE, RG_WIDTH), F32)],
        compiler_params=_params("parallel", "arbitrary"),
        name="rglru",
    )(z3, z3, z3, z3, conv_w, conv_b[None, :], wa, ba[None, :], wx, bx[None, :],
      lam[None, :], gn[None, :])


def _rope64(x, cos, sin, first_half):
    swapped = jnp.where(first_half, pltpu.roll(x, LANE - MLA_ROPE // 2, axis=1),
                        pltpu.roll(x, MLA_ROPE // 2, axis=1))
    return x * cos + swapped * sin


def _mla_proj_kernel(cq0_ref, cq1_ref, ckv_ref, kr_ref, cos_ref, sin_ref, qn_ref, kvn_ref,
                     wqn_ref, wqr_ref, wkn_ref, wv_ref, q_out, kn_out, kr_out, v_out):
    cos, sin = cos_ref[...], sin_ref[...]
    tm = cos.shape[0]
    lane = lax.broadcasted_iota(jnp.int32, (tm, LANE), 1)
    first_half = (lane % MLA_ROPE) < (MLA_ROPE // 2)
    low = lane < MLA_ROPE

    hq = _rms(jnp.concatenate([cq0_ref[...], cq1_ref[...]], axis=1), qn_ref[...]).astype(BF16)
    qn = jnp.dot(hq, wqn_ref[...], preferred_element_type=F32) * MLA_Q_SCALE
    qr = jnp.dot(hq, wqr_ref[...], preferred_element_type=F32) * MLA_Q_SCALE
    for pair in range(MLA_HEADS // 2):
        rot = _rope64(qr[:, pair * LANE:(pair + 1) * LANE], cos, sin, first_half)
        for odd in range(2):
            h = 2 * pair + odd
            piece = pltpu.roll(rot, MLA_ROPE, axis=1) if odd else rot
            q_out[h, :, 0:MLA_NOPE] = qn[:, h * MLA_NOPE:(h + 1) * MLA_NOPE].astype(BF16)
            q_out[h, :, MLA_NOPE:MLA_QK_PAD] = jnp.where(low, piece, 0.0).astype(BF16)

    hkv = _rms(ckv_ref[...], kvn_ref[...]).astype(BF16)
    kn = jnp.dot(hkv, wkn_ref[...], preferred_element_type=F32)
    v = jnp.dot(hkv, wv_ref[...], preferred_element_type=F32)
    for h in range(MLA_HEADS):
        hs = slice(h * LANE, (h + 1) * LANE)
        kn_out[h] = kn[:, hs].astype(BF16)
        v_out[h] = v[:, hs].astype(BF16)
    kr_out[...] = _rope64(kr_ref[...], cos, sin, first_half).astype(BF16)


def _mla_proj(z, kr, cos64, sin64, q_norm, kv_norm, wqn, wqr, wkn, wv, b, s):
    t = z.shape[0]
    tm = 512
    per_b = s // tm
    half = MLA_Q_RANK // 2
    cq_col = (SWA_OUT + 2 * SWA_KV_OUT + 2 * RG_WIDTH) // half
    tok = lambda i: (i, 0)
    head_blk = lambda i: (i // per_b, 0, i % per_b, 0)
    full = lambda shape: pl.BlockSpec(shape, lambda i: (0, 0))
    return pl.pallas_call(
        _mla_proj_kernel,
        grid=(t // tm,),
        in_specs=[pl.BlockSpec((tm, half), lambda i: (i, cq_col)),
                  pl.BlockSpec((tm, half), lambda i: (i, cq_col + 1)),
                  pl.BlockSpec((tm, MLA_KV_RANK), lambda i: (i, cq_col + 2)),
                  pl.BlockSpec((tm, LANE), tok), pl.BlockSpec((tm, LANE), tok),
                  pl.BlockSpec((tm, LANE), tok),
                  full((1, MLA_Q_RANK)), full((1, MLA_KV_RANK)),
                  full(wqn.shape), full(wqr.shape), full(wkn.shape), full(wv.shape)],
        out_specs=[pl.BlockSpec((None, MLA_HEADS, tm, MLA_QK_PAD), head_blk),
                   pl.BlockSpec((None, MLA_HEADS, tm, LANE), head_blk),
                   pl.BlockSpec((None, tm, LANE), lambda i: (i // per_b, i % per_b, 0)),
                   pl.BlockSpec((None, MLA_HEADS, tm, LANE), head_blk)],
        out_shape=[jax.ShapeDtypeStruct((b, MLA_HEADS, s, MLA_QK_PAD), BF16),
                   jax.ShapeDtypeStruct((b, MLA_HEADS, s, LANE), BF16),
                   jax.ShapeDtypeStruct((b, s, LANE), BF16),
                   jax.ShapeDtypeStruct((b, MLA_HEADS, s, LANE), BF16)],
        compiler_params=_params("parallel"),
        name="mla_proj",
    )(z, z, z, kr, cos64, sin64, q_norm[None, :], kv_norm[None, :], wqn, wqr, wkn, wv)


MLA_TQ = 512
MLA_HEAD_GROUP = 4


def _mla_attn_kernel(qi_tab, kj_tab, q_ref, kn_ref, kr_ref, v_ref, gn_ref, o_ref,
                     m_ref, l_ref, acc_ref):
    tq = MLA_TQ
    step = pl.program_id(1)
    qi = qi_tab[step]
    kj = kj_tab[step]

    @pl.when(kj == 0)
    def _():
        m_ref[...] = jnp.full(m_ref.shape, NEG_INF, F32)
        l_ref[...] = jnp.zeros_like(l_ref)
        acc_ref[...] = jnp.zeros_like(acc_ref)

    kr = kr_ref[...]

    def attend(masked):
        if masked:
            row = lax.broadcasted_iota(jnp.int32, (tq, tq), 0)
            col = lax.broadcasted_iota(jnp.int32, (tq, tq), 1)
            causal = col <= row
        for h0 in range(0, MLA_HEADS, MLA_HEAD_GROUP):
            heads = range(h0, h0 + MLA_HEAD_GROUP)
            scores = []
            for h in heads:
                k = jnp.concatenate([kn_ref[h], kr], axis=1)
                scores.append(lax.dot_general(q_ref[h], k, (((1,), (1,)), ((), ())),
                                              preferred_element_type=F32))
            probs, alphas = [], []
            for h, s in zip(heads, scores):
                if masked:
                    s = jnp.where(causal, s, NEG_INF)
                m_old = m_ref[h]
                m_new = jnp.maximum(m_old, jnp.max(s, axis=-1, keepdims=True))
                alpha = jnp.exp2(m_old - m_new)
                p = jnp.exp2(s - jnp.concatenate([m_new] * (tq // LANE), axis=1))
                l_ref[h] = alpha * l_ref[h] + sum(p[:, c:c + LANE] for c in range(0, tq, LANE))
                m_ref[h] = m_new
                probs.append(p.astype(BF16))
                alphas.append(alpha)
            for h, p, alpha in zip(heads, probs, alphas):
                acc_ref[h] = alpha * acc_ref[h] + jnp.dot(p, v_ref[h], preferred_element_type=F32)

    @pl.when(kj != qi)
    def _():
        attend(False)

    @pl.when(kj == qi)
    def _():
        attend(True)

    @pl.when(kj == qi)
    def _():
        sq = jnp.zeros((tq, MLA_V), F32)
        for h in range(MLA_HEADS):
            o = acc_ref[h] / jnp.sum(l_ref[h], axis=-1, keepdims=True)
            acc_ref[h] = o
            sq = sq + o * o
        inv = lax.rsqrt(jnp.sum(sq, axis=-1, keepdims=True) / MLA_OUT + NORM_EPS)
        for h in range(MLA_HEADS):
            hs = slice(h * MLA_V, (h + 1) * MLA_V)
            o_ref[:, hs] = (acc_ref[h] * inv * gn_ref[:, hs]).astype(o_ref.dtype)


def _mla_attn(q, kn, kr, v, gn):
    b, _, s, _ = q.shape
    tq = MLA_TQ
    nq = s // tq
    pairs = [(qi, kj) for qi in range(nq) for kj in range(qi + 1)]
    qi_tab = jnp.asarray(np.array([pr[0] for pr in pairs], np.int32))
    kj_tab = jnp.asarray(np.array([pr[1] for pr in pairs], np.int32))
    head_rows = lambda tab_of: (lambda bi, st, qt, kt: (bi, 0, tab_of(qt, kt)[st], 0))
    by_q = lambda qt, kt: qt
    by_k = lambda qt, kt: kt
    grid_spec = pltpu.PrefetchScalarGridSpec(
        num_scalar_prefetch=2,
        grid=(b, len(pairs)),
        in_specs=[pl.BlockSpec((None, MLA_HEADS, tq, MLA_QK_PAD), head_rows(by_q)),
                  pl.BlockSpec((None, MLA_HEADS, tq, LANE), head_rows(by_k)),
                  pl.BlockSpec((None, tq, LANE), lambda bi, st, qt, kt: (bi, kt[st], 0)),
                  pl.BlockSpec((None, MLA_HEADS, tq, LANE), head_rows(by_k)),
                  pl.BlockSpec((1, MLA_OUT), lambda bi, st, qt, kt: (0, 0))],
        out_specs=pl.BlockSpec((None, tq, MLA_OUT), lambda bi, st, qt, kt: (bi, qt[st], 0)),
        scratch_shapes=[pltpu.VMEM((MLA_HEADS, tq, LANE), F32),
                        pltpu.VMEM((MLA_HEADS, tq, LANE), F32),
                        pltpu.VMEM((MLA_HEADS, tq, MLA_V), F32)],
    )
    return pl.pallas_call(
        _mla_attn_kernel,
        grid_spec=grid_spec,
        out_shape=jax.ShapeDtypeStruct((b, s, MLA_OUT), BF16),
        compiler_params=_params("parallel", "arbitrary"),
        name="mla_attn",
    )(qi_tab, kj_tab, q, kn, kr, v, gn[None, :])


def kernel(x, p, positions, pre_mix_norm, w_in, swa_sinks, rg_conv_w, rg_conv_b, rg_gate_a_w,
           rg_gate_a_b, rg_gate_x_w, rg_gate_x_b, rg_lambda, mla_q_norm, mla_w_uq, mla_kv_norm,
           mla_w_ukv, group_norm, w_out, post_mix_norm, pre_ffn_norm, w_gate, w_up, w_down,
           post_ffn_norm, w_ple, ple_norm, w_ple_gate, b_ple_gate):
    b, s, d = x.shape
    t = b * s
    depth = w_in.shape[0]

    pos_col = positions.astype(F32).reshape(t, 1)
    cos128, sin128 = _rope_tables(pos_col, HEAD_DIM)
    cos64, sin64 = _rope_tables(pos_col, MLA_ROPE)
    cos128_3, sin128_3 = cos128.reshape(b, s, LANE), sin128.reshape(b, s, LANE)

    w_in_t = jnp.swapaxes(w_in, 1, 2)
    xf = x.reshape(t, d)
    h = _norm_cast(xf, pre_mix_norm[0])
    for i in range(depth):
        z = _matmul_wstream([h], [w_in_t], i, 1024, 1024, n_cols=Z_MAIN, w_transposed=True,
                            name="in_proj")
        w_kr = jnp.pad(w_in_t[i, Z_MAIN:, :], ((0, LANE - MLA_ROPE), (0, 0)))[None]
        kr = _matmul([h], w_kr, 0, 1024, LANE, w_transposed=True, name="in_proj_kr")
        z3 = z.reshape(b, s, Z_MAIN)
        gn = group_norm[i]

        o_a = _swa(z3, cos128_3, sin128_3, swa_sinks[i], gn[:SWA_OUT])
        o_b = _rglru(z3, rg_conv_w[i], rg_conv_b[i], rg_gate_a_w[i].astype(BF16), rg_gate_a_b[i],
                     rg_gate_x_w[i].astype(BF16), rg_gate_x_b[i], rg_lambda[i],
                     gn[SWA_OUT:SWA_OUT + RG_WIDTH])
        w_uq = mla_w_uq[i].astype(BF16).reshape(MLA_Q_RANK, MLA_HEADS, MLA_NOPE + MLA_ROPE)
        w_ukv = mla_w_ukv[i].astype(BF16).reshape(MLA_KV_RANK, MLA_HEADS, MLA_NOPE + MLA_V)
        q_c, kn_c, kr_c, v_c = _mla_proj(
            z, kr, cos64, sin64, mla_q_norm[i], mla_kv_norm[i],
            w_uq[:, :, :MLA_NOPE].reshape(MLA_Q_RANK, MLA_HEADS * MLA_NOPE),
            w_uq[:, :, MLA_NOPE:].reshape(MLA_Q_RANK, MLA_HEADS * MLA_ROPE),
            w_ukv[:, :, :MLA_NOPE].reshape(MLA_KV_RANK, MLA_HEADS * MLA_NOPE),
            w_ukv[:, :, MLA_NOPE:].reshape(MLA_KV_RANK, MLA_HEADS * MLA_V), b, s)
        o_c = _mla_attn(q_c, kn_c, kr_c, v_c, gn[SWA_OUT + RG_WIDTH:])

        mix = _matmul_wstream(
            [o_a.reshape(t, SWA_OUT), o_b.reshape(t, RG_WIDTH), o_c.reshape(t, MLA_OUT)],
            [w_out], i, 1024, 1024, name="out_proj")
        xf, h = _resid_norm(xf, mix[None], post_mix_norm[i], pre_ffn_norm[i])

        gu = _matmul_wstream([h], [w_gate, w_up], i, 1024, 512, out_dtype=BF16, swiglu=True,
                             name="ffn_up")
        f = _matmul_wstream([gu], [w_down], i, 512, 512, k_sizes=[w_down.shape[1]], name="ffn_down")
        xf, xb = _resid_norm(xf, f[None], post_ffn_norm[i], None)

        gate = _matmul_wstream([xb], [w_ple_gate], i, 1024, 1024, name="ple_gate")
        g_next = pre_mix_norm[i + 1] if i + 1 < depth else None
        xf, h = _ple(xf, gate, p[i].reshape(t, PLE_DIM), w_ple[i].astype(BF16), ple_norm[i],
                     b_ple_gate[i], g_next)
    return xf.reshape(b, s, d)
```

```python
import functools
import math

import jax
import jax.numpy as jnp
import numpy as np
from jax import lax
from jax.experimental import pallas as pl
from jax.experimental.pallas import tpu as pltpu

F32 = jnp.float32
BF16 = jnp.bfloat16
PRE_NORM_DTYPE = BF16

D_MODEL = 4096
HEAD_DIM = 128
ROPE_THETA = 10000.0
NORM_EPS = 1e-6
PLE_DIM = 256
NEG_INF = -1e30

SWA_HEADS = 12
SWA_KV_HEADS = 4
SWA_GROUP = SWA_HEADS // SWA_KV_HEADS
SWA_BLOCK = 128
SWA_OUT = SWA_HEADS * HEAD_DIM
SWA_KV_OUT = SWA_KV_HEADS * HEAD_DIM

RG_WIDTH = 1024
RG_BLOCKS = 8
RG_BLOCK_DIM = RG_WIDTH // RG_BLOCKS
RG_CONV = 4
RG_C = 8.0

MLA_HEADS = 12
MLA_Q_RANK = 1024
MLA_KV_RANK = 512
MLA_NOPE = 128
MLA_ROPE = 64
MLA_V = 128
MLA_OUT = MLA_HEADS * MLA_V
MLA_QK_PAD = 256
MLA_Q_SCALE = (MLA_NOPE + MLA_ROPE) ** -0.5 * math.log2(math.e)

MIX_WIDTH = SWA_OUT + RG_WIDTH + MLA_OUT
Z_MAIN = SWA_OUT + 2 * SWA_KV_OUT + 2 * RG_WIDTH + MLA_Q_RANK + MLA_KV_RANK

V7X_VMEM_LIMIT_BYTES = 56 * 1024 * 1024
LANE = 128
SUBLANE = 8


def _params(*semantics):
    return pltpu.CompilerParams(dimension_semantics=semantics,
                                vmem_limit_bytes=V7X_VMEM_LIMIT_BYTES)


def _rms(xf, g):
    ms = jnp.mean(xf * xf, axis=-1, keepdims=True)
    return xf * lax.rsqrt(ms + NORM_EPS) * g


def _rope_table_kernel(pos_ref, inv_ref, sign_ref, cos_ref, sin_ref):
    ang = pos_ref[...] * inv_ref[...]
    cos_ref[...] = jnp.cos(ang)
    sin_ref[...] = jnp.sin(ang) * sign_ref[...]


def _rope_tables(pos_col, rot_dim):
    t = pos_col.shape[0]
    half = rot_dim // 2
    inv = ROPE_THETA ** (-jnp.arange(0, rot_dim, 2, dtype=F32) / rot_dim)
    inv_row = jnp.tile(inv, LANE // half)[None, :]
    sign_row = jnp.tile(jnp.concatenate([-jnp.ones((half,), F32), jnp.ones((half,), F32)]),
                        LANE // rot_dim)[None, :]
    tm = 1024
    row = pl.BlockSpec((1, LANE), lambda i: (0, 0))
    return pl.pallas_call(
        _rope_table_kernel,
        grid=(t // tm,),
        in_specs=[pl.BlockSpec((tm, 1), lambda i: (i, 0)), row, row],
        out_specs=[pl.BlockSpec((tm, LANE), lambda i: (i, 0))] * 2,
        out_shape=[jax.ShapeDtypeStruct((t, LANE), F32)] * 2,
        compiler_params=_params("parallel"),
        name="rope_tables",
    )(pos_col, inv_row, sign_row)


def _norm_cast_kernel(x_ref, g_ref, o_ref):
    o_ref[...] = _rms(x_ref[...], g_ref[...]).astype(o_ref.dtype)


def _norm_cast(x, g):
    t, d = x.shape
    tm = 256
    return pl.pallas_call(
        _norm_cast_kernel,
        grid=(t // tm,),
        in_specs=[pl.BlockSpec((tm, d), lambda i: (i, 0)),
                  pl.BlockSpec((1, d), lambda i: (0, 0))],
        out_specs=pl.BlockSpec((tm, d), lambda i: (i, 0)),
        out_shape=jax.ShapeDtypeStruct((t, d), BF16),
        compiler_params=_params("parallel"),
        name="norm_cast",
    )(x, g[None, :])


def _mm_kernel(*refs, k_sizes, w_transposed):
    n_in = len(k_sizes)
    w_ref, o_ref, wb_ref = refs[n_in], refs[n_in + 1], refs[n_in + 2]

    @pl.when(pl.program_id(2) == 0)
    def _():
        wb_ref[...] = w_ref[...].astype(BF16)

    acc = None
    off = 0
    for x_ref, ks in zip(refs[:n_in], k_sizes):
        if w_transposed:
            part = lax.dot_general(x_ref[...], wb_ref[:, off:off + ks], (((1,), (1,)), ((), ())),
                                   preferred_element_type=F32)
        else:
            part = jnp.dot(x_ref[...], wb_ref[off:off + ks, :], preferred_element_type=F32)
        acc = part if acc is None else acc + part
        off += ks
    o_ref[...] = acc.astype(o_ref.dtype)


def _matmul(xs, w_stack, layer, tm, tn, n_cols=None, k_chunks=1, out_dtype=F32,
            w_transposed=False, name="matmul"):
    m = xs[0].shape[0]
    if w_transposed:
        _, n, k = w_stack.shape
    else:
        _, k, n = w_stack.shape
    n_cols = n if n_cols is None else n_cols
    k_sizes = tuple(x.shape[1] for x in xs)
    assert m % tm == 0 and n_cols % tn == 0 and k % k_chunks == 0
    tk = k // k_chunks
    if k_chunks == 1:
        assert sum(k_sizes) == k
        in_specs = [pl.BlockSpec((tm, ks), lambda j, c, i: (i, 0)) for ks in k_sizes]
        out_specs = pl.BlockSpec((tm, tn), lambda j, c, i: (i, j))
        out_shape = jax.ShapeDtypeStruct((m, n_cols), out_dtype)
    else:
        assert k_sizes == (k,)
        k_sizes = (tk,)
        in_specs = [pl.BlockSpec((tm, tk), lambda j, c, i: (i, c))]
        out_specs = pl.BlockSpec((None, tm, tn), lambda j, c, i: (c, i, j))
        out_shape = jax.ShapeDtypeStruct((k_chunks, m, n_cols), out_dtype)
    if w_transposed:
        w_block = (tn, tk)
        in_specs.append(pl.BlockSpec((None, tn, tk), lambda j, c, i: (layer, j, c)))
    else:
        w_block = (tk, tn)
        in_specs.append(pl.BlockSpec((None, tk, tn), lambda j, c, i: (layer, c, j)))
    return pl.pallas_call(
        functools.partial(_mm_kernel, k_sizes=k_sizes, w_transposed=w_transposed),
        grid=(n_cols // tn, k_chunks, m // tm),
        in_specs=in_specs,
        out_specs=out_specs,
        out_shape=out_shape,
        scratch_shapes=[pltpu.VMEM(w_block, BF16)],
        compiler_params=_params("arbitrary", "arbitrary", "arbitrary"),
        name=name,
    )(*xs, w_stack)


def _ffn_up_kernel(x_ref, wg_ref, wu_ref, o_ref, wgb_ref, wub_ref):
    @pl.when(pl.program_id(1) == 0)
    def _():
        wgb_ref[...] = wg_ref[...].astype(BF16)
        wub_ref[...] = wu_ref[...].astype(BF16)

    x = x_ref[...]
    g = jnp.dot(x, wgb_ref[...], preferred_element_type=F32)
    u = jnp.dot(x, wub_ref[...], preferred_element_type=F32)
    o_ref[...] = (g * jax.nn.sigmoid(g) * u).astype(o_ref.dtype)


def _ffn_up(h, wg_stack, wu_stack, layer, tm, tn):
    m, k = h.shape
    n = wg_stack.shape[-1]
    assert m % tm == 0 and n % tn == 0
    wspec = pl.BlockSpec((None, k, tn), lambda j, i: (layer, 0, j))
    return pl.pallas_call(
        _ffn_up_kernel,
        grid=(n // tn, m // tm),
        in_specs=[pl.BlockSpec((tm, k), lambda j, i: (i, 0)), wspec, wspec],
        out_specs=pl.BlockSpec((tm, tn), lambda j, i: (i, j)),
        out_shape=jax.ShapeDtypeStruct((m, n), BF16),
        scratch_shapes=[pltpu.VMEM((k, tn), BF16), pltpu.VMEM((k, tn), BF16)],
        compiler_params=_params("arbitrary", "arbitrary"),
        name="ffn_up",
    )(h, wg_stack, wu_stack)


def _wstream_kernel(*refs, k_sizes, n_w, layer, w_transposed, tn, n_total, rc, n_chunks, n_blocks,
                    swiglu):
    n_x = len(k_sizes)
    x_refs = refs[:n_x]
    w_hbm = refs[n_x:n_x + n_w]
    o_ref = refs[n_x + n_w]
    wb = refs[n_x + n_w + 1:n_x + 2 * n_w + 1]
    st = refs[n_x + 2 * n_w + 1:n_x + 3 * n_w + 1]
    sem = refs[n_x + 3 * n_w + 1]
    j, i = pl.program_id(0), pl.program_id(1)
    q = j * n_chunks + i
    slot = j % 2
    shift = (tn - n_total % tn) % tn

    def col0(blk):
        return pl.multiple_of(jnp.minimum(blk * tn, n_total - tn), LANE)

    def chunk_copy(w, blk, ch, buf):
        r0 = pl.multiple_of(ch * rc, SUBLANE)
        if w_transposed:
            src = w_hbm[w].at[layer, pl.ds(pl.multiple_of(col0(blk) + r0, SUBLANE), rc), :]
        else:
            src = w_hbm[w].at[layer, pl.ds(r0, rc), pl.ds(col0(blk), tn)]
        return pltpu.make_async_copy(src, st[w].at[buf], sem.at[w, buf])

    def cast_chunk(w, ch, buf, dst_slot):
        r0 = pl.multiple_of(ch * rc, 2 * SUBLANE)
        wb[w][dst_slot, pl.ds(r0, rc), :] = st[w][buf].astype(BF16)

    @pl.when(q == 0)
    def _():
        for w in range(n_w):
            chunk_copy(w, 0, 0, 0).start()
        for ch in range(n_chunks):
            buf = ch % 2
            for w in range(n_w):
                if ch + 1 < n_chunks:
                    chunk_copy(w, 0, ch + 1, 1 - buf).start()
                chunk_copy(w, 0, ch, buf).wait()
                cast_chunk(w, ch, buf, 0)
        for w in range(n_w):
            chunk_copy(w, 0, n_chunks - 1, 1).start()

    nxt = (j + 1) % n_blocks
    pbuf = (q + 1) % 2
    pch = (i + n_chunks - 1) % n_chunks
    pblk = jnp.where(i >= 1, nxt, j)
    pslot = jnp.where(i >= 1, 1 - slot, slot)
    for w in range(n_w):
        chunk_copy(w, pblk, pch, pbuf).wait()
    for w in range(n_w):
        chunk_copy(w, nxt, i, q % 2).start()
    def product(w, c0):
        acc = None
        off = 0
        for x_ref, ks in zip(x_refs, k_sizes):
            if w_transposed:
                part = lax.dot_general(x_ref[...], wb[w][slot, c0:, off:off + ks],
                                       (((1,), (1,)), ((), ())), preferred_element_type=F32)
            else:
                part = jnp.dot(x_ref[...], wb[w][slot, off:off + ks, c0:],
                               preferred_element_type=F32)
            acc = part if acc is None else acc + part
            off += ks
        return acc

    def compute(c0):
        for w in range(n_w):
            cast_chunk(w, pch, pbuf, pslot)
        out = product(0, c0)
        if swiglu:
            out = out * jax.nn.sigmoid(out) * product(1, c0)
        o_ref[:, :tn - c0] = out.astype(o_ref.dtype)
        if c0:
            o_ref[:, tn - c0:] = jnp.zeros((o_ref.shape[0], c0), o_ref.dtype)

    if shift == 0:
        compute(0)
    else:
        @pl.when(j != n_blocks - 1)
        def _():
            compute(0)

        @pl.when(j == n_blocks - 1)
        def _():
            compute(shift)

    @pl.when(q == n_blocks * n_chunks - 1)
    def _():
        for w in range(n_w):
            chunk_copy(w, nxt, i, q % 2).wait()


def _matmul_wstream(xs, w_stacks, layer, tm, tn, n_cols=None, k_sizes=None, out_dtype=F32,
                    w_transposed=False, swiglu=False, name="matmul"):
    m = xs[0].shape[0]
    k_sizes = tuple(x.shape[1] for x in xs) if k_sizes is None else tuple(k_sizes)
    k = sum(k_sizes)
    if w_transposed:
        _, n, kw = w_stacks[0].shape
        rows, cols = tn, k
    else:
        _, kw, n = w_stacks[0].shape
        rows, cols = k, tn
    n_cols = n if n_cols is None else n_cols
    n_chunks = m // tm
    rc = rows // n_chunks
    n_blocks = -(-n_cols // tn)
    n_w = len(w_stacks)
    assert kw == k and m % tm == 0 and rows % n_chunks == 0 and rc % (2 * SUBLANE) == 0
    assert n_blocks >= 2 and n_cols >= tn and (not w_transposed or n_cols % tn == 0)
    assert n_w == (2 if swiglu else 1)
    in_specs = [pl.BlockSpec((tm, ks), lambda j, i: (i, 0)) for ks in k_sizes]
    in_specs += [pl.BlockSpec(memory_space=pl.ANY)] * n_w
    scratch = [pltpu.VMEM((2, rows, cols), BF16)] * n_w
    scratch += [pltpu.VMEM((2, rc, cols), F32)] * n_w
    scratch.append(pltpu.SemaphoreType.DMA((n_w, 2)))
    return pl.pallas_call(
        functools.partial(_wstream_kernel, k_sizes=k_sizes, n_w=n_w, layer=layer,
                          w_transposed=w_transposed, tn=tn, n_total=n_cols, rc=rc,
                          n_chunks=n_chunks, n_blocks=n_blocks, swiglu=swiglu),
        grid=(n_blocks, n_chunks),
        in_specs=in_specs,
        out_specs=pl.BlockSpec((tm, tn), lambda j, i: (i, j)),
        out_shape=jax.ShapeDtypeStruct((m, n_blocks * tn), out_dtype),
        scratch_shapes=scratch,
        compiler_params=_params("arbitrary", "arbitrary"),
        name=name,
    )(*xs, *w_stacks)


def _resid_norm_kernel(x_ref, y_ref, gp_ref, gn_ref, xo_ref, ho_ref, *, norm_next):
    y = y_ref[0].astype(F32)
    for c in range(1, y_ref.shape[0]):
        y = y + y_ref[c].astype(F32)
    xn = x_ref[...] + _rms(y, gp_ref[...])
    xo_ref[...] = xn
    if norm_next:
        ho_ref[...] = _rms(xn, gn_ref[...]).astype(ho_ref.dtype)
    else:
        ho_ref[...] = xn.astype(ho_ref.dtype)


def _resid_norm(x, y_parts, g_post, g_next):
    t, d = x.shape
    parts = y_parts.shape[0]
    tm = 256
    norm_next = g_next is not None
    if g_next is None:
        g_next = g_post
    big = pl.BlockSpec((tm, d), lambda i: (i, 0))
    row = pl.BlockSpec((1, d), lambda i: (0, 0))
    return pl.pallas_call(
        functools.partial(_resid_norm_kernel, norm_next=norm_next),
        grid=(t // tm,),
        in_specs=[big, pl.BlockSpec((parts, tm, d), lambda i: (0, i, 0)), row, row],
        out_specs=[big, big],
        out_shape=[jax.ShapeDtypeStruct((t, d), F32), jax.ShapeDtypeStruct((t, d), BF16)],
        compiler_params=_params("parallel"),
        name="resid_norm",
    )(x, y_parts, g_post[None, :], g_next[None, :])


def _ple_kernel(x_ref, gate_ref, p_ref, wp_ref, pn_ref, b_ref, gn_ref, xo_ref, *maybe_ho_ref):
    e = _rms(jnp.dot(p_ref[...].astype(BF16), wp_ref[...], preferred_element_type=F32), pn_ref[...])
    xn = x_ref[...] + jax.nn.sigmoid(gate_ref[...].astype(F32) + b_ref[...]) * e
    xo_ref[...] = xn
    for ho_ref in maybe_ho_ref:
        ho_ref[...] = _rms(xn, gn_ref[...]).astype(ho_ref.dtype)


def _ple(x, gate, p, w_ple, ple_norm, b_gate, g_next):
    t, d = x.shape
    tm = 256
    norm_next = g_next is not None
    if g_next is None:
        g_next = ple_norm
    big = pl.BlockSpec((tm, d), lambda i: (i, 0))
    row = pl.BlockSpec((1, d), lambda i: (0, 0))
    out_specs = [big, big] if norm_next else [big]
    out_shape = [jax.ShapeDtypeStruct((t, d), F32)]
    if norm_next:
        out_shape.append(jax.ShapeDtypeStruct((t, d), BF16))
    outs = pl.pallas_call(
        _ple_kernel,
        grid=(t // tm,),
        in_specs=[big, big, pl.BlockSpec((tm, PLE_DIM), lambda i: (i, 0)),
                  pl.BlockSpec((PLE_DIM, d), lambda i: (0, 0)), row, row, row],
        out_specs=out_specs,
        out_shape=out_shape,
        compiler_params=_params("parallel"),
        name="ple",
    )(x, gate, p, w_ple, ple_norm[None, :], b_gate[None, :], g_next[None, :])
    return (outs[0], outs[1]) if norm_next else (outs[0], None)


SWA_KV_PER_PASS = 2


def _swa_kernel(sink_ref, q_ref, kc_ref, vc_ref, vp_ref, cos_ref, sin_ref, gn_ref, o_ref,
                kprev_ref):
    blk = SWA_BLOCK
    n = pl.program_id(1)
    cos, sin = cos_ref[...], sin_ref[...]

    @pl.when(n == 0)
    def _():
        kprev_ref[...] = jnp.zeros_like(kprev_ref)

    def rope(x):
        return x * cos + pltpu.roll(x, HEAD_DIM // 2, axis=1) * sin

    row = lax.broadcasted_iota(jnp.int32, (blk, 2 * blk), 0)
    col = lax.broadcasted_iota(jnp.int32, (blk, 2 * blk), 1)
    dist = blk + row - col
    kpos = (n - 1) * blk + col
    valid = (dist >= 0) & (dist < blk) & (kpos >= 0)
    scale = HEAD_DIM ** -0.5

    outs = []
    sq = jnp.zeros((blk, HEAD_DIM), F32)
    for kv0 in range(0, SWA_KV_HEADS, SWA_KV_PER_PASS):
        scores, values = [], []
        for kv in range(kv0, kv0 + SWA_KV_PER_PASS):
            ks = slice(kv * HEAD_DIM, (kv + 1) * HEAD_DIM)
            k_cur = rope(kc_ref[:, ks]).astype(BF16)
            k = jnp.concatenate([kprev_ref[kv], k_cur], axis=0)
            kprev_ref[kv] = k_cur
            values.append(jnp.concatenate([vp_ref[:, ks], vc_ref[:, ks]], axis=0).astype(BF16))
            for g in range(SWA_GROUP):
                h = kv * SWA_GROUP + g
                q = rope(q_ref[:, h * HEAD_DIM:(h + 1) * HEAD_DIM]).astype(BF16)
                scores.append(lax.dot_general(q, k, (((1,), (1,)), ((), ())),
                                              preferred_element_type=F32))
        probs, denoms = [], []
        for i, s in enumerate(scores):
            s = jnp.where(valid, s * scale, NEG_INF)
            sink = sink_ref[kv0 * SWA_GROUP + i]
            m = jnp.maximum(jnp.max(s, axis=-1, keepdims=True), sink)
            p = jnp.exp(s - m)
            denoms.append(jnp.sum(p[:, :blk] + p[:, blk:], axis=-1, keepdims=True)
                          + jnp.exp(sink - m))
            probs.append(p.astype(BF16))
        for i, (p, denom) in enumerate(zip(probs, denoms)):
            o = jnp.dot(p, values[i // SWA_GROUP], preferred_element_type=F32) / denom
            outs.append(o)
            sq = sq + o * o
    inv = lax.rsqrt(jnp.sum(sq, axis=-1, keepdims=True) / SWA_OUT + NORM_EPS)
    for h, o in enumerate(outs):
        hs = slice(h * HEAD_DIM, (h + 1) * HEAD_DIM)
        o_ref[:, hs] = (o * inv * gn_ref[:, hs]).astype(o_ref.dtype)


def _swa(z3, cos, sin, sinks, gn):
    b, s, _ = z3.shape
    blk = SWA_BLOCK
    nb = s // blk
    kcol = SWA_OUT // SWA_KV_OUT
    vcol = kcol + 1
    cur = lambda bi, n: (bi, n, 0)
    tab = pl.BlockSpec((None, blk, LANE), cur)
    return pl.pallas_call(
        _swa_kernel,
        grid=(b, nb),
        in_specs=[
            pl.BlockSpec(memory_space=pltpu.SMEM),
            pl.BlockSpec((None, blk, SWA_OUT), cur),
            pl.BlockSpec((None, blk, SWA_KV_OUT), lambda bi, n: (bi, n, kcol)),
            pl.BlockSpec((None, blk, SWA_KV_OUT), lambda bi, n: (bi, n, vcol)),
            pl.BlockSpec((None, blk, SWA_KV_OUT), lambda bi, n: (bi, jnp.maximum(n - 1, 0), vcol)),
            tab, tab,
            pl.BlockSpec((1, SWA_OUT), lambda bi, n: (0, 0)),
        ],
        out_specs=pl.BlockSpec((None, blk, SWA_OUT), cur),
        out_shape=jax.ShapeDtypeStruct((b, s, SWA_OUT), BF16),
        scratch_shapes=[pltpu.VMEM((SWA_KV_HEADS, blk, HEAD_DIM), BF16)],
        compiler_params=_params("arbitrary", "arbitrary"),
        name="swa",
    )(sinks, z3, z3, z3, z3, cos, sin, gn[None, :])


RG_CHUNK = 256


def _rglru_kernel(x0_ref, x1_ref, g0_ref, g1_ref, cw_ref, cb_ref, wa_ref, ba_ref, wx_ref, bx_ref,
                  lam_ref, gn_ref, o_ref, xe_ref, a_ref, b_ref, h_ref, carry_ref):
    lc = RG_CHUNK
    pad = SUBLANE

    @pl.when(pl.program_id(1) == 0)
    def _():
        xe_ref[0:pad, :] = jnp.zeros((pad, RG_WIDTH), F32)
        carry_ref[...] = jnp.zeros_like(carry_ref)

    xe_ref[pad:pad + lc, :] = jnp.concatenate([x0_ref[...], x1_ref[...]], axis=1)
    cw = cw_ref[...]
    xc = cb_ref[...]
    for j in range(RG_CONV):
        sh = RG_CONV - 1 - j
        xc = xc + xe_ref[pad - sh:pad - sh + lc, :] * cw[j:j + 1, :]
    xe_ref[0:pad, :] = xe_ref[lc:lc + pad, :]

    xcb = xc.astype(BF16)

    def gate(w_ref, bias_ref):
        parts = [jnp.dot(xcb[:, nb * RG_BLOCK_DIM:(nb + 1) * RG_BLOCK_DIM], w_ref[nb],
                         preferred_element_type=F32) for nb in range(RG_BLOCKS)]
        return jax.nn.sigmoid(jnp.concatenate(parts, axis=1) + bias_ref[...])

    r = gate(wa_ref, ba_ref)
    i = gate(wx_ref, bx_ref)
    nlam = -lam_ref[...]
    softplus = jnp.maximum(nlam, 0.0) + jnp.log1p(jnp.exp(-jnp.abs(nlam)))
    log_a = (-RG_C * softplus) * r
    a = jnp.exp(log_a)
    one_minus_a2 = -jnp.tanh(log_a) * (a * a + 1.0)
    a_ref[...] = a
    b_ref[...] = jnp.sqrt(one_minus_a2) * i * xc

    sub = lax.broadcasted_iota(jnp.int32, (SUBLANE, RG_WIDTH), 0)

    def scan_tile(t, h):
        r0 = pl.multiple_of(t * SUBLANE, SUBLANE)
        at = a_ref[pl.ds(r0, SUBLANE), :]
        bt = b_ref[pl.ds(r0, SUBLANE), :]
        for sh in (1, 2, 4):
            keep = sub >= sh
            a_sh = pltpu.roll(at, sh, axis=0)
            b_sh = pltpu.roll(bt, sh, axis=0)
            bt = jnp.where(keep, at * b_sh + bt, bt)
            at = jnp.where(keep, at * a_sh, at)
        ht = at * h + bt
        h_ref[pl.ds(r0, SUBLANE), :] = ht
        return ht[SUBLANE - 1:SUBLANE, :]

    carry_ref[0:1, :] = lax.fori_loop(0, lc // SUBLANE, scan_tile, carry_ref[0:1, :], unroll=4)

    g = jnp.concatenate([g0_ref[...], g1_ref[...]], axis=1)
    y = h_ref[...] * jax.nn.gelu(g)
    o_ref[...] = _rms(y, gn_ref[...]).astype(o_ref.dtype)


def _rglru(z3, conv_w, conv_b, wa, ba, wx, bx, lam, gn):
    b, s, _ = z3.shape
    lc = RG_CHUNK
    half = RG_WIDTH // 2
    x_col = (SWA_OUT + 2 * SWA_KV_OUT) // half
    g_col = x_col + 2

    def zcol(c):
        return pl.BlockSpec((None, lc, half), lambda bi, ci: (bi, ci, c))

    row = pl.BlockSpec((1, RG_WIDTH), lambda bi, ci: (0, 0))
    wspec = pl.BlockSpec((RG_BLOCKS, RG_BLOCK_DIM, RG_BLOCK_DIM), lambda bi, ci: (0, 0, 0))
    return pl.pallas_call(
        _rglru_kernel,
        grid=(b, s // lc),
        in_specs=[zcol(x_col), zcol(x_col + 1), zcol(g_col), zcol(g_col + 1),
                  pl.BlockSpec((RG_CONV, RG_WIDTH), lambda bi, ci: (0, 0)), row,
                  wspec, row, wspec, row, row, row],
        out_specs=pl.BlockSpec((None, lc, RG_WIDTH), lambda bi, ci: (bi, ci, 0)),
        out_shape=jax.ShapeDtypeStruct((b, s, RG_WIDTH), BF16),
        scratch_shapes=[pltpu.VMEM((lc + SUBLANE, RG_WIDTH), F32),
                        pltpu.VMEM((lc, RG_WIDTH), F32),
                        pltpu.VMEM((lc, RG_WIDTH), F32),
                        pltpu.VMEM((lc, RG_WIDTH), F32),
                        pltpu.VMEM((SUBLANE, RG_WIDTH), F32)],
        compiler_params=_params("parallel", "arbitrary"),
        name="rglru",
    )(z3, z3, z3, z3, conv_w, conv_b[None, :], wa, ba[None, :], wx, bx[None, :],
      lam[None, :], gn[None, :])


def _rope64(x, cos, sin, first_half):
    swapped = jnp.where(first_half, pltpu.roll(x, LANE - MLA_ROPE // 2, axis=1),
                        pltpu.roll(x, MLA_ROPE // 2, axis=1))
    return x * cos + swapped * sin


def _mla_proj_kernel(cq0_ref, cq1_ref, ckv_ref, kr_ref, cos_ref, sin_ref, qn_ref, kvn_ref,
                     wqn_ref, wqr_ref, wkn_ref, wv_ref, q_out, kn_out, kr_out, v_out):
    cos, sin = cos_ref[...], sin_ref[...]
    tm = cos.shape[0]
    lane = lax.broadcasted_iota(jnp.int32, (tm, LANE), 1)
    first_half = (lane % MLA_ROPE) < (MLA_ROPE // 2)
    low = lane < MLA_ROPE

    hq = _rms(jnp.concatenate([cq0_ref[...], cq1_ref[...]], axis=1), qn_ref[...]).astype(BF16)
    qn = jnp.dot(hq, wqn_ref[...], preferred_element_type=F32) * MLA_Q_SCALE
    qr = jnp.dot(hq, wqr_ref[...], preferred_element_type=F32) * MLA_Q_SCALE
    for pair in range(MLA_HEADS // 2):
        rot = _rope64(qr[:, pair * LANE:(pair + 1) * LANE], cos, sin, first_half)
        for odd in range(2):
            h = 2 * pair + odd
            piece = pltpu.roll(rot, MLA_ROPE, axis=1) if odd else rot
            q_out[h, :, 0:MLA_NOPE] = qn[:, h * MLA_NOPE:(h + 1) * MLA_NOPE].astype(BF16)
            q_out[h, :, MLA_NOPE:MLA_QK_PAD] = jnp.where(low, piece, 0.0).astype(BF16)

    hkv = _rms(ckv_ref[...], kvn_ref[...]).astype(BF16)
    kn = jnp.dot(hkv, wkn_ref[...], preferred_element_type=F32)
    v = jnp.dot(hkv, wv_ref[...], preferred_element_type=F32)
    for h in range(MLA_HEADS):
        hs = slice(h * LANE, (h + 1) * LANE)
        kn_out[h] = kn[:, hs].astype(BF16)
        v_out[h] = v[:, hs].astype(BF16)
    kr_out[...] = _rope64(kr_ref[...], cos, sin, first_half).astype(BF16)


def _mla_proj(z, kr, cos64, sin64, q_norm, kv_norm, wqn, wqr, wkn, wv, b, s):
    t = z.shape[0]
    tm = 512
    per_b = s // tm
    half = MLA_Q_RANK // 2
    cq_col = (SWA_OUT + 2 * SWA_KV_OUT + 2 * RG_WIDTH) // half
    tok = lambda i: (i, 0)
    head_blk = lambda i: (i // per_b, 0, i % per_b, 0)
    full = lambda shape: pl.BlockSpec(shape, lambda i: (0, 0))
    return pl.pallas_call(
        _mla_proj_kernel,
        grid=(t // tm,),
        in_specs=[pl.BlockSpec((tm, half), lambda i: (i, cq_col)),
                  pl.BlockSpec((tm, half), lambda i: (i, cq_col + 1)),
                  pl.BlockSpec((tm, MLA_KV_RANK), lambda i: (i, cq_col + 2)),
                  pl.BlockSpec((tm, LANE), tok), pl.BlockSpec((tm, LANE), tok),
                  pl.BlockSpec((tm, LANE), tok),
                  full((1, MLA_Q_RANK)), full((1, MLA_KV_RANK)),
                  full(wqn.shape), full(wqr.shape), full(wkn.shape), full(wv.shape)],
        out_specs=[pl.BlockSpec((None, MLA_HEADS, tm, MLA_QK_PAD), head_blk),
                   pl.BlockSpec((None, MLA_HEADS, tm, LANE), head_blk),
                   pl.BlockSpec((None, tm, LANE), lambda i: (i // per_b, i % per_b, 0)),
                   pl.BlockSpec((None, MLA_HEADS, tm, LANE), head_blk)],
        out_shape=[jax.ShapeDtypeStruct((b, MLA_HEADS, s, MLA_QK_PAD), BF16),
                   jax.ShapeDtypeStruct((b, MLA_HEADS, s, LANE), BF16),
                   jax.ShapeDtypeStruct((b, s, LANE), BF16),
                   jax.ShapeDtypeStruct((b, MLA_HEADS, s, LANE), BF16)],
        compiler_params=_params("parallel"),
        name="mla_proj",
    )(z, z, z, kr, cos64, sin64, q_norm[None, :], kv_norm[None, :], wqn, wqr, wkn, wv)


MLA_TQ = 512
MLA_HEAD_GROUP = 4


def _mla_attn_kernel(qi_tab, kj_tab, q_ref, kn_ref, kr_ref, v_ref, gn_ref, o_ref,
                     m_ref, l_ref, acc_ref):
    tq = MLA_TQ
    step = pl.program_id(1)
    qi = qi_tab[step]
    kj = kj_tab[step]

    @pl.when(kj == 0)
    def _():
        m_ref[...] = jnp.full(m_ref.shape, NEG_INF, F32)
        l_ref[...] = jnp.zeros_like(l_ref)
        acc_ref[...] = jnp.zeros_like(acc_ref)

    kr = kr_ref[...]

    def attend(masked):
        if masked:
            row = lax.broadcasted_iota(jnp.int32, (tq, tq), 0)
            col = lax.broadcasted_iota(jnp.int32, (tq, tq), 1)
            causal = col <= row
        for h0 in range(0, MLA_HEADS, MLA_HEAD_GROUP):
            heads = range(h0, h0 + MLA_HEAD_GROUP)
            scores = []
            for h in heads:
                k = jnp.concatenate([kn_ref[h], kr], axis=1)
                scores.append(lax.dot_general(q_ref[h], k, (((1,), (1,)), ((), ())),
                                              preferred_element_type=F32))
            probs, alphas = [], []
            for h, s in zip(heads, scores):
                if masked:
                    s = jnp.where(causal, s, NEG_INF)
                m_old = m_ref[h]
                m_new = jnp.maximum(m_old, jnp.max(s, axis=-1, keepdims=True))
                alpha = jnp.exp2(m_old - m_new)
                p = jnp.exp2(s - jnp.concatenate([m_new] * (tq // LANE), axis=1))
                l_ref[h] = alpha * l_ref[h] + sum(p[:, c:c + LANE] for c in range(0, tq, LANE))
                m_ref[h] = m_new
                probs.append(p.astype(BF16))
                alphas.append(alpha)
            for h, p, alpha in zip(heads, probs, alphas):
                acc_ref[h] = alpha * acc_ref[h] + jnp.dot(p, v_ref[h], preferred_element_type=F32)

    @pl.when(kj != qi)
    def _():
        attend(False)

    @pl.when(kj == qi)
    def _():
        attend(True)

    @pl.when(kj == qi)
    def _():
        sq = jnp.zeros((tq, MLA_V), F32)
        for h in range(MLA_HEADS):
            o = acc_ref[h] / jnp.sum(l_ref[h], axis=-1, keepdims=True)
            acc_ref[h] = o
            sq = sq + o * o
        inv = lax.rsqrt(jnp.sum(sq, axis=-1, keepdims=True) / MLA_OUT + NORM_EPS)
        for h in range(MLA_HEADS):
            hs = slice(h * MLA_V, (h + 1) * MLA_V)
            o_ref[:, hs] = (acc_ref[h] * inv * gn_ref[:, hs]).astype(o_ref.dtype)


def _mla_attn(q, kn, kr, v, gn):
    b, _, s, _ = q.shape
    tq = MLA_TQ
    nq = s // tq
    pairs = [(qi, kj) for qi in range(nq) for kj in range(qi + 1)]
    qi_tab = jnp.asarray(np.array([pr[0] for pr in pairs], np.int32))
    kj_tab = jnp.asarray(np.array([pr[1] for pr in pairs], np.int32))
    head_rows = lambda tab_of: (lambda bi, st, qt, kt: (bi, 0, tab_of(qt, kt)[st], 0))
    by_q = lambda qt, kt: qt
    by_k = lambda qt, kt: kt
    grid_spec = pltpu.PrefetchScalarGridSpec(
        num_scalar_prefetch=2,
        grid=(b, len(pairs)),
        in_specs=[pl.BlockSpec((None, MLA_HEADS, tq, MLA_QK_PAD), head_rows(by_q)),
                  pl.BlockSpec((None, MLA_HEADS, tq, LANE), head_rows(by_k)),
                  pl.BlockSpec((None, tq, LANE), lambda bi, st, qt, kt: (bi, kt[st], 0)),
                  pl.BlockSpec((None, MLA_HEADS, tq, LANE), head_rows(by_k)),
                  pl.BlockSpec((1, MLA_OUT), lambda bi, st, qt, kt: (0, 0))],
        out_specs=pl.BlockSpec((None, tq, MLA_OUT), lambda bi, st, qt, kt: (bi, qt[st], 0)),
        scratch_shapes=[pltpu.VMEM((MLA_HEADS, tq, LANE), F32),
                        pltpu.VMEM((MLA_HEADS, tq, LANE), F32),
                        pltpu.VMEM((MLA_HEADS, tq, MLA_V), F32)],
    )
    return pl.pallas_call(
        _mla_attn_kernel,
        grid_spec=grid_spec,
        out_shape=jax.ShapeDtypeStruct((b, s, MLA_OUT), BF16),
        compiler_params=_params("parallel", "arbitrary"),
        name="mla_attn",
    )(qi_tab, kj_tab, q, kn, kr, v, gn[None, :])


def kernel(x, p, positions, pre_mix_norm, w_in, swa_sinks, rg_conv_w, rg_conv_b, rg_gate_a_w,
           rg_gate_a_b, rg_gate_x_w, rg_gate_x_b, rg_lambda, mla_q_norm, mla_w_uq, mla_kv_norm,
           mla_w_ukv, group_norm, w_out, post_mix_norm, pre_ffn_norm, w_gate, w_up, w_down,
           post_ffn_norm, w_ple, ple_norm, w_ple_gate, b_ple_gate):
    b, s, d = x.shape
    t = b * s
    depth = w_in.shape[0]

    pos_col = positions.astype(F32).reshape(t, 1)
    cos128, sin128 = _rope_tables(pos_col, HEAD_DIM)
    cos64, sin64 = _rope_tables(pos_col, MLA_ROPE)
    cos128_3, sin128_3 = cos128.reshape(b, s, LANE), sin128.reshape(b, s, LANE)

    w_in_t = jnp.swapaxes(w_in, 1, 2)
    xf = x.reshape(t, d)
    h = _norm_cast(xf, pre_mix_norm[0])
    for i in range(depth):
        z = _matmul_wstream([h], [w_in_t], i, 1024, 1024, n_cols=Z_MAIN, w_transposed=True,
                            name="in_proj")
        w_kr = jnp.pad(w_in_t[i, Z_MAIN:, :], ((0, LANE - MLA_ROPE), (0, 0)))[None]
        kr = _matmul([h], w_kr, 0, 1024, LANE, w_transposed=True, name="in_proj_kr")
        z3 = z.reshape(b, s, Z_MAIN)
        gn = group_norm[i]

        o_a = _swa(z3, cos128_3, sin128_3, swa_sinks[i], gn[:SWA_OUT])
        o_b = _rglru(z3, rg_conv_w[i], rg_conv_b[i], rg_gate_a_w[i].astype(BF16), rg_gate_a_b[i],
                     rg_gate_x_w[i].astype(BF16), rg_gate_x_b[i], rg_lambda[i],
                     gn[SWA_OUT:SWA_OUT + RG_WIDTH])
        w_uq = mla_w_uq[i].astype(BF16).reshape(MLA_Q_RANK, MLA_HEADS, MLA_NOPE + MLA_ROPE)
        w_ukv = mla_w_ukv[i].astype(BF16).reshape(MLA_KV_RANK, MLA_HEADS, MLA_NOPE + MLA_V)
        q_c, kn_c, kr_c, v_c = _mla_proj(
            z, kr, cos64, sin64, mla_q_norm[i], mla_kv_norm[i],
            w_uq[:, :, :MLA_NOPE].reshape(MLA_Q_RANK, MLA_HEADS * MLA_NOPE),
            w_uq[:, :, MLA_NOPE:].reshape(MLA_Q_RANK, MLA_HEADS * MLA_ROPE),
            w_ukv[:, :, :MLA_NOPE].reshape(MLA_KV_RANK, MLA_HEADS * MLA_NOPE),
            w_ukv[:, :, MLA_NOPE:].reshape(MLA_KV_RANK, MLA_HEADS * MLA_V), b, s)
        o_c = _mla_attn(q_c, kn_c, kr_c, v_c, gn[SWA_OUT + RG_WIDTH:])

        mix = _matmul_wstream(
            [o_a.reshape(t, SWA_OUT), o_b.reshape(t, RG_WIDTH), o_c.reshape(t, MLA_OUT)],
            [w_out], i, 1024, 1024, out_dtype=PRE_NORM_DTYPE, name="out_proj")
        xf, h = _resid_norm(xf, mix[None], post_mix_norm[i], pre_ffn_norm[i])

        gu = _matmul_wstream([h], [w_gate, w_up], i, 1024, 512, out_dtype=BF16, swiglu=True,
                             name="ffn_up")
        f = _matmul_wstream([gu], [w_down], i, 512, 512, k_sizes=[w_down.shape[1]],
                            out_dtype=PRE_NORM_DTYPE, name="ffn_down")
        xf, xb = _resid_norm(xf, f[None], post_ffn_norm[i], None)

        gate = _matmul_wstream([xb], [w_ple_gate], i, 1024, 1024, out_dtype=PRE_NORM_DTYPE,
                               name="ple_gate")
        g_next = pre_mix_norm[i + 1] if i + 1 < depth else None
        xf, h = _ple(xf, gate, p[i].reshape(t, PLE_DIM), w_ple[i].astype(BF16), ple_norm[i],
                     b_ple_gate[i], g_next)
    return xf.reshape(b, s, d)
```

```python
import functools
import math

import jax
import jax.numpy as jnp
import numpy as np
from jax import lax
from jax.experimental import pallas as pl
from jax.experimental.pallas import tpu as pltpu

F32 = jnp.float32
BF16 = jnp.bfloat16
PRE_NORM_DTYPE = BF16

HEAD_DIM = 128
ROPE_THETA = 10000.0
NORM_EPS = 1e-6
PLE_DIM = 256
NEG_INF = -1e30

SWA_HEADS = 12
SWA_KV_HEADS = 4
SWA_GROUP = SWA_HEADS // SWA_KV_HEADS
SWA_BLOCK = 128
SWA_OUT = SWA_HEADS * HEAD_DIM
SWA_KV_OUT = SWA_KV_HEADS * HEAD_DIM

RG_WIDTH = 1024
RG_BLOCKS = 8
RG_BLOCK_DIM = RG_WIDTH // RG_BLOCKS
RG_CONV = 4
RG_C = 8.0

MLA_HEADS = 12
MLA_Q_RANK = 1024
MLA_KV_RANK = 512
MLA_NOPE = 128
MLA_ROPE = 64
MLA_V = 128
MLA_OUT = MLA_HEADS * MLA_V
MLA_QK_PAD = 256
MLA_Q_SCALE = (MLA_NOPE + MLA_ROPE) ** -0.5 * math.log2(math.e)

Z_MAIN = SWA_OUT + 2 * SWA_KV_OUT + 2 * RG_WIDTH + MLA_Q_RANK + MLA_KV_RANK

V7X_VMEM_LIMIT_BYTES = 56 * 1024 * 1024
LANE = 128
SUBLANE = 8


def _params(*semantics):
    return pltpu.CompilerParams(dimension_semantics=semantics,
                                vmem_limit_bytes=V7X_VMEM_LIMIT_BYTES)


def _rms(xf, g):
    ms = jnp.mean(xf * xf, axis=-1, keepdims=True)
    return xf * lax.rsqrt(ms + NORM_EPS) * g


def _rope_table_kernel(pos_ref, inv_ref, sign_ref, cos_ref, sin_ref):
    ang = pos_ref[...] * inv_ref[...]
    cos_ref[...] = jnp.cos(ang)
    sin_ref[...] = jnp.sin(ang) * sign_ref[...]


def _rope_tables(pos_col, rot_dim):
    t = pos_col.shape[0]
    half = rot_dim // 2
    inv = ROPE_THETA ** (-jnp.arange(0, rot_dim, 2, dtype=F32) / rot_dim)
    inv_row = jnp.tile(inv, LANE // half)[None, :]
    sign_row = jnp.tile(jnp.concatenate([-jnp.ones((half,), F32), jnp.ones((half,), F32)]),
                        LANE // rot_dim)[None, :]
    tm = 1024
    row = pl.BlockSpec((1, LANE), lambda i: (0, 0))
    return pl.pallas_call(
        _rope_table_kernel,
        grid=(t // tm,),
        in_specs=[pl.BlockSpec((tm, 1), lambda i: (i, 0)), row, row],
        out_specs=[pl.BlockSpec((tm, LANE), lambda i: (i, 0))] * 2,
        out_shape=[jax.ShapeDtypeStruct((t, LANE), F32)] * 2,
        compiler_params=_params("parallel"),
        name="rope_tables",
    )(pos_col, inv_row, sign_row)


def _norm_cast_kernel(x_ref, g_ref, o_ref):
    o_ref[...] = _rms(x_ref[...], g_ref[...]).astype(o_ref.dtype)


def _norm_cast(x, g):
    t, d = x.shape
    tm = 256
    return pl.pallas_call(
        _norm_cast_kernel,
        grid=(t // tm,),
        in_specs=[pl.BlockSpec((tm, d), lambda i: (i, 0)),
                  pl.BlockSpec((1, d), lambda i: (0, 0))],
        out_specs=pl.BlockSpec((tm, d), lambda i: (i, 0)),
        out_shape=jax.ShapeDtypeStruct((t, d), BF16),
        compiler_params=_params("parallel"),
        name="norm_cast",
    )(x, g[None, :])


def _mm_kernel(*refs, k_sizes, w_transposed):
    n_in = len(k_sizes)
    w_ref, o_ref, wb_ref = refs[n_in], refs[n_in + 1], refs[n_in + 2]

    @pl.when(pl.program_id(2) == 0)
    def _():
        wb_ref[...] = w_ref[...].astype(BF16)

    acc = None
    off = 0
    for x_ref, ks in zip(refs[:n_in], k_sizes):
        if w_transposed:
            part = lax.dot_general(x_ref[...], wb_ref[:, off:off + ks], (((1,), (1,)), ((), ())),
                                   preferred_element_type=F32)
        else:
            part = jnp.dot(x_ref[...], wb_ref[off:off + ks, :], preferred_element_type=F32)
        acc = part if acc is None else acc + part
        off += ks
    o_ref[...] = acc.astype(o_ref.dtype)


def _matmul(xs, w_stack, layer, tm, tn, n_cols=None, k_chunks=1, out_dtype=F32,
            w_transposed=False, name="matmul"):
    m = xs[0].shape[0]
    if w_transposed:
        _, n, k = w_stack.shape
    else:
        _, k, n = w_stack.shape
    n_cols = n if n_cols is None else n_cols
    k_sizes = tuple(x.shape[1] for x in xs)
    assert m % tm == 0 and n_cols % tn == 0 and k % k_chunks == 0
    tk = k // k_chunks
    if k_chunks == 1:
        assert sum(k_sizes) == k
        in_specs = [pl.BlockSpec((tm, ks), lambda j, c, i: (i, 0)) for ks in k_sizes]
        out_specs = pl.BlockSpec((tm, tn), lambda j, c, i: (i, j))
        out_shape = jax.ShapeDtypeStruct((m, n_cols), out_dtype)
    else:
        assert k_sizes == (k,)
        k_sizes = (tk,)
        in_specs = [pl.BlockSpec((tm, tk), lambda j, c, i: (i, c))]
        out_specs = pl.BlockSpec((None, tm, tn), lambda j, c, i: (c, i, j))
        out_shape = jax.ShapeDtypeStruct((k_chunks, m, n_cols), out_dtype)
    if w_transposed:
        w_block = (tn, tk)
        in_specs.append(pl.BlockSpec((None, tn, tk), lambda j, c, i: (layer, j, c)))
    else:
        w_block = (tk, tn)
        in_specs.append(pl.BlockSpec((None, tk, tn), lambda j, c, i: (layer, c, j)))
    return pl.pallas_call(
        functools.partial(_mm_kernel, k_sizes=k_sizes, w_transposed=w_transposed),
        grid=(n_cols // tn, k_chunks, m // tm),
        in_specs=in_specs,
        out_specs=out_specs,
        out_shape=out_shape,
        scratch_shapes=[pltpu.VMEM(w_block, BF16)],
        compiler_params=_params("arbitrary", "arbitrary", "arbitrary"),
        name=name,
    )(*xs, w_stack)


def _wstream_kernel(*refs, k_sizes, n_w, layer, w_transposed, tn, n_total, rc, n_chunks, n_blocks,
                    swiglu):
    n_x = len(k_sizes)
    x_refs = refs[:n_x]
    w_hbm = refs[n_x:n_x + n_w]
    o_ref = refs[n_x + n_w]
    wb = refs[n_x + n_w + 1:n_x + 2 * n_w + 1]
    st = refs[n_x + 2 * n_w + 1:n_x + 3 * n_w + 1]
    sem = refs[n_x + 3 * n_w + 1]
    j, i = pl.program_id(0), pl.program_id(1)
    q = j * n_chunks + i
    slot = j % 2
    shift = (tn - n_total % tn) % tn

    def col0(blk):
        return pl.multiple_of(jnp.minimum(blk * tn, n_total - tn), LANE)

    def chunk_copy(w, blk, ch, buf):
        r0 = pl.multiple_of(ch * rc, SUBLANE)
        if w_transposed:
            src = w_hbm[w].at[layer, pl.ds(pl.multiple_of(col0(blk) + r0, SUBLANE), rc), :]
        else:
            src = w_hbm[w].at[layer, pl.ds(r0, rc), pl.ds(col0(blk), tn)]
        return pltpu.make_async_copy(src, st[w].at[buf], sem.at[w, buf])

    def cast_chunk(w, ch, buf, dst_slot):
        r0 = pl.multiple_of(ch * rc, 2 * SUBLANE)
        wb[w][dst_slot, pl.ds(r0, rc), :] = st[w][buf].astype(BF16)

    @pl.when(q == 0)
    def _():
        for w in range(n_w):
            chunk_copy(w, 0, 0, 0).start()
        for ch in range(n_chunks):
            buf = ch % 2
            for w in range(n_w):
                if ch + 1 < n_chunks:
                    chunk_copy(w, 0, ch + 1, 1 - buf).start()
                chunk_copy(w, 0, ch, buf).wait()
                cast_chunk(w, ch, buf, 0)
        for w in range(n_w):
            chunk_copy(w, 0, n_chunks - 1, 1).start()

    nxt = (j + 1) % n_blocks
    pbuf = (q + 1) % 2
    pch = (i + n_chunks - 1) % n_chunks
    pblk = jnp.where(i >= 1, nxt, j)
    pslot = jnp.where(i >= 1, 1 - slot, slot)
    for w in range(n_w):
        chunk_copy(w, pblk, pch, pbuf).wait()
    for w in range(n_w):
        chunk_copy(w, nxt, i, q % 2).start()
    def product(w, c0):
        acc = None
        off = 0
        for x_ref, ks in zip(x_refs, k_sizes):
            if w_transposed:
                part = lax.dot_general(x_ref[...], wb[w][slot, c0:, off:off + ks],
                                       (((1,), (1,)), ((), ())), preferred_element_type=F32)
            else:
                part = jnp.dot(x_ref[...], wb[w][slot, off:off + ks, c0:],
                               preferred_element_type=F32)
            acc = part if acc is None else acc + part
            off += ks
        return acc

    def compute(c0):
        for w in range(n_w):
            cast_chunk(w, pch, pbuf, pslot)
        out = product(0, c0)
        if swiglu:
            out = out * jax.nn.sigmoid(out) * product(1, c0)
        o_ref[:, :tn - c0] = out.astype(o_ref.dtype)
        if c0:
            o_ref[:, tn - c0:] = jnp.zeros((o_ref.shape[0], c0), o_ref.dtype)

    if shift == 0:
        compute(0)
    else:
        @pl.when(j != n_blocks - 1)
        def _():
            compute(0)

        @pl.when(j == n_blocks - 1)
        def _():
            compute(shift)

    @pl.when(q == n_blocks * n_chunks - 1)
    def _():
        for w in range(n_w):
            chunk_copy(w, nxt, i, q % 2).wait()


def _matmul_wstream(xs, w_stacks, layer, tm, tn, n_cols=None, k_sizes=None, out_dtype=F32,
                    w_transposed=False, swiglu=False, name="matmul"):
    m = xs[0].shape[0]
    k_sizes = tuple(x.shape[1] for x in xs) if k_sizes is None else tuple(k_sizes)
    k = sum(k_sizes)
    if w_transposed:
        _, n, kw = w_stacks[0].shape
        rows, cols = tn, k
    else:
        _, kw, n = w_stacks[0].shape
        rows, cols = k, tn
    n_cols = n if n_cols is None else n_cols
    n_chunks = m // tm
    rc = rows // n_chunks
    n_blocks = -(-n_cols // tn)
    n_w = len(w_stacks)
    assert kw == k and m % tm == 0 and rows % n_chunks == 0 and rc % (2 * SUBLANE) == 0
    assert n_blocks >= 2 and n_cols >= tn and (not w_transposed or n_cols % tn == 0)
    assert n_w == (2 if swiglu else 1)
    in_specs = [pl.BlockSpec((tm, ks), lambda j, i: (i, 0)) for ks in k_sizes]
    in_specs += [pl.BlockSpec(memory_space=pl.ANY)] * n_w
    scratch = [pltpu.VMEM((2, rows, cols), BF16)] * n_w
    scratch += [pltpu.VMEM((2, rc, cols), F32)] * n_w
    scratch.append(pltpu.SemaphoreType.DMA((n_w, 2)))
    return pl.pallas_call(
        functools.partial(_wstream_kernel, k_sizes=k_sizes, n_w=n_w, layer=layer,
                          w_transposed=w_transposed, tn=tn, n_total=n_cols, rc=rc,
                          n_chunks=n_chunks, n_blocks=n_blocks, swiglu=swiglu),
        grid=(n_blocks, n_chunks),
        in_specs=in_specs,
        out_specs=pl.BlockSpec((tm, tn), lambda j, i: (i, j)),
        out_shape=jax.ShapeDtypeStruct((m, n_blocks * tn), out_dtype),
        scratch_shapes=scratch,
        compiler_params=_params("arbitrary", "arbitrary"),
        name=name,
    )(*xs, *w_stacks)


def _resid_norm_kernel(x_ref, y_ref, gp_ref, gn_ref, xo_ref, ho_ref, *, norm_next):
    y = y_ref[0].astype(F32)
    for c in range(1, y_ref.shape[0]):
        y = y + y_ref[c].astype(F32)
    xn = x_ref[...] + _rms(y, gp_ref[...])
    xo_ref[...] = xn
    if norm_next:
        ho_ref[...] = _rms(xn, gn_ref[...]).astype(ho_ref.dtype)
    else:
        ho_ref[...] = xn.astype(ho_ref.dtype)


def _resid_norm(x, y_parts, g_post, g_next):
    t, d = x.shape
    parts = y_parts.shape[0]
    tm = 256
    norm_next = g_next is not None
    if g_next is None:
        g_next = g_post
    big = pl.BlockSpec((tm, d), lambda i: (i, 0))
    row = pl.BlockSpec((1, d), lambda i: (0, 0))
    return pl.pallas_call(
        functools.partial(_resid_norm_kernel, norm_next=norm_next),
        grid=(t // tm,),
        in_specs=[big, pl.BlockSpec((parts, tm, d), lambda i: (0, i, 0)), row, row],
        out_specs=[big, big],
        out_shape=[jax.ShapeDtypeStruct((t, d), F32), jax.ShapeDtypeStruct((t, d), BF16)],
        compiler_params=_params("parallel"),
        name="resid_norm",
    )(x, y_parts, g_post[None, :], g_next[None, :])


def _ple_kernel(x_ref, gate_ref, p_ref, wp_ref, pn_ref, b_ref, gn_ref, xo_ref, *maybe_ho_ref):
    e = _rms(jnp.dot(p_ref[...].astype(BF16), wp_ref[...], preferred_element_type=F32), pn_ref[...])
    xn = x_ref[...] + jax.nn.sigmoid(gate_ref[...].astype(F32) + b_ref[...]) * e
    xo_ref[...] = xn
    for ho_ref in maybe_ho_ref:
        ho_ref[...] = _rms(xn, gn_ref[...]).astype(ho_ref.dtype)


def _ple(x, gate, p, w_ple, ple_norm, b_gate, g_next):
    t, d = x.shape
    tm = 256
    norm_next = g_next is not None
    if g_next is None:
        g_next = ple_norm
    big = pl.BlockSpec((tm, d), lambda i: (i, 0))
    row = pl.BlockSpec((1, d), lambda i: (0, 0))
    out_specs = [big, big] if norm_next else [big]
    out_shape = [jax.ShapeDtypeStruct((t, d), F32)]
    if norm_next:
        out_shape.append(jax.ShapeDtypeStruct((t, d), BF16))
    outs = pl.pallas_call(
        _ple_kernel,
        grid=(t // tm,),
        in_specs=[big, big, pl.BlockSpec((tm, PLE_DIM), lambda i: (i, 0)),
                  pl.BlockSpec((PLE_DIM, d), lambda i: (0, 0)), row, row, row],
        out_specs=out_specs,
        out_shape=out_shape,
        compiler_params=_params("parallel"),
        name="ple",
    )(x, gate, p, w_ple, ple_norm[None, :], b_gate[None, :], g_next[None, :])
    return (outs[0], outs[1]) if norm_next else (outs[0], None)


SWA_KV_PER_PASS = 2


def _swa_kernel(sink_ref, q_ref, kc_ref, vc_ref, vp_ref, cos_ref, sin_ref, gn_ref, o_ref,
                kprev_ref):
    blk = SWA_BLOCK
    n = pl.program_id(1)
    cos, sin = cos_ref[...], sin_ref[...]

    @pl.when(n == 0)
    def _():
        kprev_ref[...] = jnp.zeros_like(kprev_ref)

    def rope(x):
        return x * cos + pltpu.roll(x, HEAD_DIM // 2, axis=1) * sin

    row = lax.broadcasted_iota(jnp.int32, (blk, 2 * blk), 0)
    col = lax.broadcasted_iota(jnp.int32, (blk, 2 * blk), 1)
    dist = blk + row - col
    kpos = (n - 1) * blk + col
    valid = (dist >= 0) & (dist < blk) & (kpos >= 0)
    scale = HEAD_DIM ** -0.5

    outs = []
    sq = jnp.zeros((blk, HEAD_DIM), F32)
    for kv0 in range(0, SWA_KV_HEADS, SWA_KV_PER_PASS):
        scores, values = [], []
        for kv in range(kv0, kv0 + SWA_KV_PER_PASS):
            ks = slice(kv * HEAD_DIM, (kv + 1) * HEAD_DIM)
            k_cur = rope(kc_ref[:, ks]).astype(BF16)
            k = jnp.concatenate([kprev_ref[kv], k_cur], axis=0)
            kprev_ref[kv] = k_cur
            values.append(jnp.concatenate([vp_ref[:, ks], vc_ref[:, ks]], axis=0).astype(BF16))
            for g in range(SWA_GROUP):
                h = kv * SWA_GROUP + g
                q = rope(q_ref[:, h * HEAD_DIM:(h + 1) * HEAD_DIM]).astype(BF16)
                scores.append(lax.dot_general(q, k, (((1,), (1,)), ((), ())),
                                              preferred_element_type=F32))
        probs, denoms = [], []
        for i, s in enumerate(scores):
            s = jnp.where(valid, s * scale, NEG_INF)
            sink = sink_ref[kv0 * SWA_GROUP + i]
            m = jnp.maximum(jnp.max(s, axis=-1, keepdims=True), sink)
            p = jnp.exp(s - m)
            denoms.append(jnp.sum(p[:, :blk] + p[:, blk:], axis=-1, keepdims=True)
                          + jnp.exp(sink - m))
            probs.append(p.astype(BF16))
        for i, (p, denom) in enumerate(zip(probs, denoms)):
            o = jnp.dot(p, values[i // SWA_GROUP], preferred_element_type=F32) / denom
            outs.append(o)
            sq = sq + o * o
    inv = lax.rsqrt(jnp.sum(sq, axis=-1, keepdims=True) / SWA_OUT + NORM_EPS)
    for h, o in enumerate(outs):
        hs = slice(h * HEAD_DIM, (h + 1) * HEAD_DIM)
        o_ref[:, hs] = (o * inv * gn_ref[:, hs]).astype(o_ref.dtype)


def _swa(z3, cos, sin, sinks, gn):
    b, s, _ = z3.shape
    blk = SWA_BLOCK
    nb = s // blk
    kcol = SWA_OUT // SWA_KV_OUT
    vcol = kcol + 1
    cur = lambda bi, n: (bi, n, 0)
    tab = pl.BlockSpec((None, blk, LANE), cur)
    return pl.pallas_call(
        _swa_kernel,
        grid=(b, nb),
        in_specs=[
            pl.BlockSpec(memory_space=pltpu.SMEM),
            pl.BlockSpec((None, blk, SWA_OUT), cur),
            pl.BlockSpec((None, blk, SWA_KV_OUT), lambda bi, n: (bi, n, kcol)),
            pl.BlockSpec((None, blk, SWA_KV_OUT), lambda bi, n: (bi, n, vcol)),
            pl.BlockSpec((None, blk, SWA_KV_OUT), lambda bi, n: (bi, jnp.maximum(n - 1, 0), vcol)),
            tab, tab,
            pl.BlockSpec((1, SWA_OUT), lambda bi, n: (0, 0)),
        ],
        out_specs=pl.BlockSpec((None, blk, SWA_OUT), cur),
        out_shape=jax.ShapeDtypeStruct((b, s, SWA_OUT), BF16),
        scratch_shapes=[pltpu.VMEM((SWA_KV_HEADS, blk, HEAD_DIM), BF16)],
        compiler_params=_params("arbitrary", "arbitrary"),
        name="swa",
    )(sinks, z3, z3, z3, z3, cos, sin, gn[None, :])


RG_CHUNK = 256


def _rglru_kernel(x0_ref, x1_ref, g0_ref, g1_ref, cw_ref, cb_ref, wa_ref, ba_ref, wx_ref, bx_ref,
                  lam_ref, gn_ref, o_ref, xe_ref, a_ref, b_ref, h_ref, carry_ref):
    lc = RG_CHUNK
    pad = SUBLANE

    @pl.when(pl.program_id(1) == 0)
    def _():
        xe_ref[0:pad, :] = jnp.zeros((pad, RG_WIDTH), F32)
        carry_ref[...] = jnp.zeros_like(carry_ref)

    xe_ref[pad:pad + lc, :] = jnp.concatenate([x0_ref[...], x1_ref[...]], axis=1)
    cw = cw_ref[...]
    xc = cb_ref[...]
    for j in range(RG_CONV):
        sh = RG_CONV - 1 - j
        xc = xc + xe_ref[pad - sh:pad - sh + lc, :] * cw[j:j + 1, :]
    xe_ref[0:pad, :] = xe_ref[lc:lc + pad, :]

    xcb = xc.astype(BF16)

    def gate(w_ref, bias_ref):
        parts = [jnp.dot(xcb[:, nb * RG_BLOCK_DIM:(nb + 1) * RG_BLOCK_DIM], w_ref[nb],
                         preferred_element_type=F32) for nb in range(RG_BLOCKS)]
        return jax.nn.sigmoid(jnp.concatenate(parts, axis=1) + bias_ref[...])

    r = gate(wa_ref, ba_ref)
    i = gate(wx_ref, bx_ref)
    nlam = -lam_ref[...]
    softplus = jnp.maximum(nlam, 0.0) + jnp.log1p(jnp.exp(-jnp.abs(nlam)))
    log_a = (-RG_C * softplus) * r
    a = jnp.exp(log_a)
    one_minus_a2 = -jnp.tanh(log_a) * (a * a + 1.0)
    a_ref[...] = a
    b_ref[...] = jnp.sqrt(one_minus_a2) * i * xc

    sub = lax.broadcasted_iota(jnp.int32, (SUBLANE, RG_WIDTH), 0)

    def scan_tile(t, h):
        r0 = pl.multiple_of(t * SUBLANE, SUBLANE)
        at = a_ref[pl.ds(r0, SUBLANE), :]
        bt = b_ref[pl.ds(r0, SUBLANE), :]
        for sh in (1, 2, 4):
            keep = sub >= sh
            a_sh = pltpu.roll(at, sh, axis=0)
            b_sh = pltpu.roll(bt, sh, axis=0)
            bt = jnp.where(keep, at * b_sh + bt, bt)
            at = jnp.where(keep, at * a_sh, at)
        ht = at * h + bt
        h_ref[pl.ds(r0, SUBLANE), :] = ht
        return ht[SUBLANE - 1:SUBLANE, :]

    carry_ref[0:1, :] = lax.fori_loop(0, lc // SUBLANE, scan_tile, carry_ref[0:1, :], unroll=4)

    g = jnp.concatenate([g0_ref[...], g1_ref[...]], axis=1)
    y = h_ref[...] * jax.nn.gelu(g)
    o_ref[...] = _rms(y, gn_ref[...]).astype(o_ref.dtype)


def _rglru(z3, conv_w, conv_b, wa, ba, wx, bx, lam, gn):
    b, s, _ = z3.shape
    lc = RG_CHUNK
    half = RG_WIDTH // 2
    x_col = (SWA_OUT + 2 * SWA_KV_OUT) // half
    g_col = x_col + 2

    def zcol(c):
        return pl.BlockSpec((None, lc, half), lambda bi, ci: (bi, ci, c))

    row = pl.BlockSpec((1, RG_WIDTH), lambda bi, ci: (0, 0))
    wspec = pl.BlockSpec((RG_BLOCKS, RG_BLOCK_DIM, RG_BLOCK_DIM), lambda bi, ci: (0, 0, 0))
    return pl.pallas_call(
        _rglru_kernel,
        grid=(b, s // lc),
        in_specs=[zcol(x_col), zcol(x_col + 1), zcol(g_col), zcol(g_col + 1),
                  pl.BlockSpec((RG_CONV, RG_WIDTH), lambda bi, ci: (0, 0)), row,
                  wspec, row, wspec, row, row, row],
        out_specs=pl.BlockSpec((None, lc, RG_WIDTH), lambda bi, ci: (bi, ci, 0)),
        out_shape=jax.ShapeDtypeStruct((b, s, RG_WIDTH), BF16),
        scratch_shapes=[pltpu.VMEM((lc + SUBLANE, RG_WIDTH), F32),
                        pltpu.VMEM((lc, RG_WIDTH), F32),
                        pltpu.VMEM((lc, RG_WIDTH), F32),
                        pltpu.VMEM((lc, RG_WIDTH), F32),
                        pltpu.VMEM((SUBLANE, RG_WIDTH), F32)],
        compiler_params=_params("parallel", "arbitrary"),
        name="rglru",
    )(z3, z3, z3, z3, conv_w, conv_b[None, :], wa, ba[None, :], wx, bx[None, :],
      lam[None, :], gn[None, :])


def _rope64(x, cos, sin, first_half):
    swapped = jnp.where(first_half, pltpu.roll(x, LANE - MLA_ROPE // 2, axis=1),
                        pltpu.roll(x, MLA_ROPE // 2, axis=1))
    return x * cos + swapped * sin


def _mla_proj_kernel(cq0_ref, cq1_ref, ckv_ref, kr_ref, cos_ref, sin_ref, qn_ref, kvn_ref,
                     wqn_ref, wqr_ref, wkn_ref, wv_ref, q_out, kn_out, kr_out, v_out):
    cos, sin = cos_ref[...], sin_ref[...]
    tm = cos.shape[0]
    lane = lax.broadcasted_iota(jnp.int32, (tm, LANE), 1)
    first_half = (lane % MLA_ROPE) < (MLA_ROPE // 2)
    low = lane < MLA_ROPE

    hq = _rms(jnp.concatenate([cq0_ref[...], cq1_ref[...]], axis=1), qn_ref[...]).astype(BF16)
    qn = jnp.dot(hq, wqn_ref[...], preferred_element_type=F32) * MLA_Q_SCALE
    qr = jnp.dot(hq, wqr_ref[...], preferred_element_type=F32) * MLA_Q_SCALE
    for pair in range(MLA_HEADS // 2):
        rot = _rope64(qr[:, pair * LANE:(pair + 1) * LANE], cos, sin, first_half)
        for odd in range(2):
            h = 2 * pair + odd
            piece = pltpu.roll(rot, MLA_ROPE, axis=1) if odd else rot
            q_out[h, :, 0:MLA_NOPE] = qn[:, h * MLA_NOPE:(h + 1) * MLA_NOPE].astype(BF16)
            q_out[h, :, MLA_NOPE:MLA_QK_PAD] = jnp.where(low, piece, 0.0).astype(BF16)

    hkv = _rms(ckv_ref[...], kvn_ref[...]).astype(BF16)
    kn = jnp.dot(hkv, wkn_ref[...], preferred_element_type=F32)
    v = jnp.dot(hkv, wv_ref[...], preferred_element_type=F32)
    for h in range(MLA_HEADS):
        hs = slice(h * LANE, (h + 1) * LANE)
        kn_out[h] = kn[:, hs].astype(BF16)
        v_out[h] = v[:, hs].astype(BF16)
    kr_out[...] = _rope64(kr_ref[...], cos, sin, first_half).astype(BF16)


def _mla_proj(z, kr, cos64, sin64, q_norm, kv_norm, wqn, wqr, wkn, wv, b, s):
    t = z.shape[0]
    tm = 512
    per_b = s // tm
    half = MLA_Q_RANK // 2
    cq_col = (SWA_OUT + 2 * SWA_KV_OUT + 2 * RG_WIDTH) // half
    tok = lambda i: (i, 0)
    head_blk = lambda i: (i // per_b, 0, i % per_b, 0)
    full = lambda shape: pl.BlockSpec(shape, lambda i: (0, 0))
    return pl.pallas_call(
        _mla_proj_kernel,
        grid=(t // tm,),
        in_specs=[pl.BlockSpec((tm, half), lambda i: (i, cq_col)),
                  pl.BlockSpec((tm, half), lambda i: (i, cq_col + 1)),
                  pl.BlockSpec((tm, MLA_KV_RANK), lambda i: (i, cq_col + 2)),
                  pl.BlockSpec((tm, LANE), tok), pl.BlockSpec((tm, LANE), tok),
                  pl.BlockSpec((tm, LANE), tok),
                  full((1, MLA_Q_RANK)), full((1, MLA_KV_RANK)),
                  full(wqn.shape), full(wqr.shape), full(wkn.shape), full(wv.shape)],
        out_specs=[pl.BlockSpec((None, MLA_HEADS, tm, MLA_QK_PAD), head_blk),
                   pl.BlockSpec((None, MLA_HEADS, tm, LANE), head_blk),
                   pl.BlockSpec((None, tm, LANE), lambda i: (i // per_b, i % per_b, 0)),
                   pl.BlockSpec((None, MLA_HEADS, tm, LANE), head_blk)],
        out_shape=[jax.ShapeDtypeStruct((b, MLA_HEADS, s, MLA_QK_PAD), BF16),
                   jax.ShapeDtypeStruct((b, MLA_HEADS, s, LANE), BF16),
                   jax.ShapeDtypeStruct((b, s, LANE), BF16),
                   jax.ShapeDtypeStruct((b, MLA_HEADS, s, LANE), BF16)],
        compiler_params=_params("parallel"),
        name="mla_proj",
    )(z, z, z, kr, cos64, sin64, q_norm[None, :], kv_norm[None, :], wqn, wqr, wkn, wv)


MLA_TQ = 512
MLA_HEAD_GROUP = 4


def _mla_attn_kernel(qi_tab, kj_tab, q_ref, kn_ref, kr_ref, v_ref, gn_ref, o_ref,
                     m_ref, l_ref, acc_ref):
    tq = MLA_TQ
    step = pl.program_id(1)
    qi = qi_tab[step]
    kj = kj_tab[step]

    @pl.when(kj == 0)
    def _():
        m_ref[...] = jnp.full(m_ref.shape, NEG_INF, F32)
        l_ref[...] = jnp.zeros_like(l_ref)
        acc_ref[...] = jnp.zeros_like(acc_ref)

    kr = kr_ref[...]

    def attend(masked):
        if masked:
            row = lax.broadcasted_iota(jnp.int32, (tq, tq), 0)
            col = lax.broadcasted_iota(jnp.int32, (tq, tq), 1)
            causal = col <= row
        for h0 in range(0, MLA_HEADS, MLA_HEAD_GROUP):
            heads = range(h0, h0 + MLA_HEAD_GROUP)
            scores = []
            for h in heads:
                k = jnp.concatenate([kn_ref[h], kr], axis=1)
                scores.append(lax.dot_general(q_ref[h], k, (((1,), (1,)), ((), ())),
                                              preferred_element_type=F32))
            probs, alphas = [], []
            for h, s in zip(heads, scores):
                if masked:
                    s = jnp.where(causal, s, NEG_INF)
                m_old = m_ref[h]
                m_new = jnp.maximum(m_old, jnp.max(s, axis=-1, keepdims=True))
                alpha = jnp.exp2(m_old - m_new)
                p = jnp.exp2(s - jnp.concatenate([m_new] * (tq // LANE), axis=1))
                l_ref[h] = alpha * l_ref[h] + sum(p[:, c:c + LANE] for c in range(0, tq, LANE))
                m_ref[h] = m_new
                probs.append(p.astype(BF16))
                alphas.append(alpha)
            for h, p, alpha in zip(heads, probs, alphas):
                acc_ref[h] = alpha * acc_ref[h] + jnp.dot(p, v_ref[h], preferred_element_type=F32)

    @pl.when(kj != qi)
    def _():
        attend(False)

    @pl.when(kj == qi)
    def _():
        attend(True)

    @pl.when(kj == qi)
    def _():
        sq = jnp.zeros((tq, MLA_V), F32)
        for h in range(MLA_HEADS):
            o = acc_ref[h] / jnp.sum(l_ref[h], axis=-1, keepdims=True)
            acc_ref[h] = o
            sq = sq + o * o
        inv = lax.rsqrt(jnp.sum(sq, axis=-1, keepdims=True) / MLA_OUT + NORM_EPS)
        for h in range(MLA_HEADS):
            hs = slice(h * MLA_V, (h + 1) * MLA_V)
            o_ref[:, hs] = (acc_ref[h] * inv * gn_ref[:, hs]).astype(o_ref.dtype)


def _mla_attn(q, kn, kr, v, gn):
    b, _, s, _ = q.shape
    tq = MLA_TQ
    nq = s // tq
    pairs = [(qi, kj) for qi in range(nq) for kj in range(qi + 1)]
    qi_tab = jnp.asarray(np.array([pr[0] for pr in pairs], np.int32))
    kj_tab = jnp.asarray(np.array([pr[1] for pr in pairs], np.int32))
    head_rows = lambda tab_of: (lambda bi, st, qt, kt: (bi, 0, tab_of(qt, kt)[st], 0))
    by_q = lambda qt, kt: qt
    by_k = lambda qt, kt: kt
    grid_spec = pltpu.PrefetchScalarGridSpec(
        num_scalar_prefetch=2,
        grid=(b, len(pairs)),
        in_specs=[pl.BlockSpec((None, MLA_HEADS, tq, MLA_QK_PAD), head_rows(by_q)),
                  pl.BlockSpec((None, MLA_HEADS, tq, LANE), head_rows(by_k)),
                  pl.BlockSpec((None, tq, LANE), lambda bi, st, qt, kt: (bi, kt[st], 0)),
                  pl.BlockSpec((None, MLA_HEADS, tq, LANE), head_rows(by_k)),
                  pl.BlockSpec((1, MLA_OUT), lambda bi, st, qt, kt: (0, 0))],
        out_specs=pl.BlockSpec((None, tq, MLA_OUT), lambda bi, st, qt, kt: (bi, qt[st], 0)),
        scratch_shapes=[pltpu.VMEM((MLA_HEADS, tq, LANE), F32),
                        pltpu.VMEM((MLA_HEADS, tq, LANE), F32),
                        pltpu.VMEM((MLA_HEADS, tq, MLA_V), F32)],
    )
    return pl.pallas_call(
        _mla_attn_kernel,
        grid_spec=grid_spec,
        out_shape=jax.ShapeDtypeStruct((b, s, MLA_OUT), BF16),
        compiler_params=_params("parallel", "arbitrary"),
        name="mla_attn",
    )(qi_tab, kj_tab, q, kn, kr, v, gn[None, :])


def kernel(x, p, positions, pre_mix_norm, w_in, swa_sinks, rg_conv_w, rg_conv_b, rg_gate_a_w,
           rg_gate_a_b, rg_gate_x_w, rg_gate_x_b, rg_lambda, mla_q_norm, mla_w_uq, mla_kv_norm,
           mla_w_ukv, group_norm, w_out, post_mix_norm, pre_ffn_norm, w_gate, w_up, w_down,
           post_ffn_norm, w_ple, ple_norm, w_ple_gate, b_ple_gate):
    b, s, d = x.shape
    t = b * s
    depth = w_in.shape[0]

    pos_col = positions.astype(F32).reshape(t, 1)
    cos128, sin128 = _rope_tables(pos_col, HEAD_DIM)
    cos64, sin64 = _rope_tables(pos_col, MLA_ROPE)
    cos128_3, sin128_3 = cos128.reshape(b, s, LANE), sin128.reshape(b, s, LANE)

    w_in_t = jnp.swapaxes(w_in, 1, 2)
    xf = x.reshape(t, d)
    h = _norm_cast(xf, pre_mix_norm[0])
    for i in range(depth):
        z = _matmul_wstream([h], [w_in_t], i, 1024, 1024, n_cols=Z_MAIN, w_transposed=True,
                            name="in_proj")
        w_kr = jnp.pad(w_in_t[i, Z_MAIN:, :], ((0, LANE - MLA_ROPE), (0, 0)))[None]
        kr = _matmul([h], w_kr, 0, 1024, LANE, w_transposed=True, name="in_proj_kr")
        z3 = z.reshape(b, s, Z_MAIN)
        gn = group_norm[i]

        o_a = _swa(z3, cos128_3, sin128_3, swa_sinks[i], gn[:SWA_OUT])
        o_b = _rglru(z3, rg_conv_w[i], rg_conv_b[i], rg_gate_a_w[i].astype(BF16), rg_gate_a_b[i],
                     rg_gate_x_w[i].astype(BF16), rg_gate_x_b[i], rg_lambda[i],
                     gn[SWA_OUT:SWA_OUT + RG_WIDTH])
        w_uq = mla_w_uq[i].astype(BF16).reshape(MLA_Q_RANK, MLA_HEADS, MLA_NOPE + MLA_ROPE)
        w_ukv = mla_w_ukv[i].astype(BF16).reshape(MLA_KV_RANK, MLA_HEADS, MLA_NOPE + MLA_V)
        q_c, kn_c, kr_c, v_c = _mla_proj(
            z, kr, cos64, sin64, mla_q_norm[i], mla_kv_norm[i],
            w_uq[:, :, :MLA_NOPE].reshape(MLA_Q_RANK, MLA_HEADS * MLA_NOPE),
            w_uq[:, :, MLA_NOPE:].reshape(MLA_Q_RANK, MLA_HEADS * MLA_ROPE),
            w_ukv[:, :, :MLA_NOPE].reshape(MLA_KV_RANK, MLA_HEADS * MLA_NOPE),
            w_ukv[:, :, MLA_NOPE:].reshape(MLA_KV_RANK, MLA_HEADS * MLA_V), b, s)
        o_c = _mla_attn(q_c, kn_c, kr_c, v_c, gn[SWA_OUT + RG_WIDTH:])

        mix = _matmul_wstream(
            [o_a.reshape(t, SWA_OUT), o_b.reshape(t, RG_WIDTH), o_c.reshape(t, MLA_OUT)],
            [w_out], i, 1024, 1024, out_dtype=PRE_NORM_DTYPE, name="out_proj")
        xf, h = _resid_norm(xf, mix[None], post_mix_norm[i], pre_ffn_norm[i])

        gu = _matmul_wstream([h], [w_gate, w_up], i, 1024, 512, out_dtype=BF16, swiglu=True,
                             name="ffn_up")
        f = _matmul_wstream([gu], [w_down], i, 512, 512, k_sizes=[w_down.shape[1]],
                            out_dtype=PRE_NORM_DTYPE, name="ffn_down")
        xf, xb = _resid_norm(xf, f[None], post_ffn_norm[i], None)

        gate = _matmul_wstream([xb], [w_ple_gate], i, 1024, 1024, out_dtype=PRE_NORM_DTYPE,
                               name="ple_gate")
        g_next = pre_mix_norm[i + 1] if i + 1 < depth else None
        xf, h = _ple(xf, gate, p[i].reshape(t, PLE_DIM), w_ple[i].astype(BF16), ple_norm[i],
                     b_ple_gate[i], g_next)
    return xf.reshape(b, s, d)
```
